```python
import jax, jax.numpy as jnp
from jax import lax
import numpy as np

D_MODEL = 1024
BATCH = 8
SEQ = 4096
DEPTH = 2
DEC_BATCH = 1
DEC_SEQ = 16384
PAST_LEN = 128

GRID_W = 64
N_HEADS = 16
HEAD_DIM = D_MODEL // N_HEADS
MAX_KH = 8
KW = 16
KB = min(2 * KW, GRID_W)
CONV_W = 3
D_FF = 2816
N_MIXERS = 2
N_NA = (DEPTH + 1) // 2
N_SC = DEPTH // 2
EPS = 1e-6
NEG_INF = -1e30

kernel_name = "hybrid_natten_shortconv_encoder"


def rmsnorm(x, g):
    x32 = x.astype(jnp.float32)
    y = x32 * lax.rsqrt(jnp.mean(x32 * x32, axis=-1, keepdims=True) + EPS)
    return (y * g.astype(jnp.float32)).astype(x.dtype)


def conv3_centered(x, w):
    xp = jnp.pad(x, ((0, 0), (1, 1), (0, 0)))
    return xp[:, :-2] * w[0] + xp[:, 1:-1] * w[1] + xp[:, 2:] * w[2]


def neighbourhood_attention(q, k, v, rpb):
    bsz, t, h, dh = q.shape
    rows = t // GRID_W
    kh = min(MAX_KH, rows)
    ncb = GRID_W // KW
    scale = dh ** -0.5
    qg = q.reshape(bsz, rows, GRID_W, h, dh).transpose(0, 3, 1, 2, 4)
    kg = k.reshape(bsz, rows, GRID_W, h, dh).transpose(0, 3, 1, 2, 4)
    vg = v.reshape(bsz, rows, GRID_W, h, dh).transpose(0, 3, 1, 2, 4)
    cols = np.arange(GRID_W)
    col_start = np.clip(cols - KW // 2, 0, GRID_W - KW).reshape(ncb, KW)
    q_cols = cols.reshape(ncb, KW)
    blk_start = np.clip(np.arange(ncb) * KW - KW // 2, 0, GRID_W - KB)
    key_cols = blk_start[:, None] + np.arange(KB)[None, :]
    kc = key_cols[:, None, :]
    col_valid = (kc >= col_start[..., None]) & (kc < col_start[..., None] + KW)
    dc_idx = np.clip(kc - q_cols[..., None] + KW - 1, 0, 2 * KW - 2)
    mask = jnp.asarray(col_valid)[None, None, :, :, None, :]

    def row_block(r):
        rs = jnp.clip(r - kh // 2, 0, rows - kh)
        qr = lax.dynamic_index_in_dim(qg, r, axis=2, keepdims=False)
        kband = lax.dynamic_slice_in_dim(kg, rs, kh, axis=2)
        vband = lax.dynamic_slice_in_dim(vg, rs, kh, axis=2)
        kblk = kband[:, :, :, key_cols]
        vblk = vband[:, :, :, key_cols]
        qb = qr.reshape(bsz, h, ncb, KW, dh)
        s = jnp.einsum('bhjqd,bhijkd->bhjqik', qb, kblk,
                       preferred_element_type=jnp.float32) * scale
        dr = rs + jnp.arange(kh) - r
        rpb_r = rpb[:, dr + MAX_KH - 1]
        bias = rpb_r[:, :, dc_idx].transpose(0, 2, 3, 1, 4)
        s = jnp.where(mask, s + bias.astype(jnp.float32)[None], NEG_INF)
        p = jax.nn.softmax(s, axis=(-2, -1))
        o = jnp.einsum('bhjqik,bhijkd->bhjqd', p.astype(vblk.dtype), vblk)
        return o.reshape(bsz, h, GRID_W, dh)

    out = lax.map(row_block, jnp.arange(rows))
    return out.transpose(1, 0, 3, 2, 4).reshape(bsz, t, h * dh)


def na_mixer(x, w_qkv, b_qkv, rpb, w_o):
    bsz, t, _ = x.shape
    qkv = jnp.dot(x, w_qkv) + b_qkv
    q, k, v = jnp.split(qkv, 3, axis=-1)
    shp = (bsz, t, N_HEADS, HEAD_DIM)
    o = neighbourhood_attention(q.reshape(shp), k.reshape(shp), v.reshape(shp), rpb)
    return jnp.dot(o, w_o)


def short_conv_mixer(x, w_in, conv_w, w_out):
    bg, cg, u = jnp.split(jnp.dot(x, w_in), 3, axis=-1)
    y = conv3_centered(cg * u, conv_w)
    return jnp.dot(bg * y, w_out)


def conv_ffn(x, w_up, conv_w, conv_b, w_down):
    a, val = jnp.split(jnp.dot(x, w_up), 2, axis=-1)
    a = conv3_centered(a, conv_w) + conv_b
    return jnp.dot(jax.nn.silu(a) * val, w_down)


def trunk(x, c, p):
    for i in range(DEPTH):
        mod = jnp.dot(jax.nn.silu(c), p['ada_w'][i]) + p['ada_b'][i]
        sh1, sc1, g1, sh2, sc2, g2 = jnp.split(mod[:, None, :], 6, axis=-1)
        h = rmsnorm(x, p['norm1_g'][i]) * (1 + sc1) + sh1
        j = i // N_MIXERS
        if i % N_MIXERS == 0:
            m = na_mixer(h, p['na_w_qkv'][j], p['na_b_qkv'][j], p['na_rpb'][j], p['na_w_o'][j])
        else:
            m = short_conv_mixer(h, p['sc_w_in'][j], p['sc_conv_w'][j], p['sc_w_out'][j])
        x = x + g1 * m
        h = rmsnorm(x, p['norm2_g'][i]) * (1 + sc2) + sh2
        x = x + g2 * conv_ffn(h, p['ffn_w_up'][i], p['ffn_conv_w'][i], p['ffn_conv_b'][i], p['ffn_w_down'][i])
    return rmsnorm(x, p['final_g'])


def setup_inputs(seed: int = 0) -> dict:
    key = jax.random.key(seed)
    ks = jax.random.split(key, 20)
    D, F = D_MODEL, D_FF
    nrm = jax.random.normal
    return {
        'x_prompt': nrm(ks[0], (BATCH, SEQ, D), jnp.float32),
        'x_sample': nrm(ks[1], (DEC_BATCH, DEC_SEQ, D), jnp.float32),
        'c_prompt': nrm(ks[2], (BATCH, D), jnp.float32),
        'c_sample': nrm(ks[3], (DEC_BATCH, D), jnp.float32),
        'ada_w': nrm(ks[4], (DEPTH, D, 6 * D), jnp.float32) * (0.5 * D ** -0.5),
        'ada_b': nrm(ks[5], (DEPTH, 6 * D), jnp.float32) * 0.02,
        'norm1_g': 1.0 + 0.02 * nrm(ks[6], (DEPTH, D), jnp.float32),
        'norm2_g': 1.0 + 0.02 * nrm(ks[7], (DEPTH, D), jnp.float32),
        'na_w_qkv': nrm(ks[8], (N_NA, D, 3 * D), jnp.float32) * D ** -0.5,
        'na_b_qkv': nrm(ks[9], (N_NA, 3 * D), jnp.float32) * 0.02,
        'na_rpb': nrm(ks[10], (N_NA, N_HEADS, 2 * MAX_KH - 1, 2 * KW - 1), jnp.float32) * 0.1,
        'na_w_o': nrm(ks[11], (N_NA, D, D), jnp.float32) * D ** -0.5,
        'sc_w_in': nrm(ks[12], (N_SC, D, 3 * D), jnp.float32) * D ** -0.5,
        'sc_conv_w': nrm(ks[13], (N_SC, CONV_W, D), jnp.float32) * CONV_W ** -0.5,
        'sc_w_out': nrm(ks[14], (N_SC, D, D), jnp.float32) * D ** -0.5,
        'ffn_w_up': nrm(ks[15], (DEPTH, D, 2 * F), jnp.float32) * D ** -0.5,
        'ffn_conv_w': nrm(ks[16], (DEPTH, CONV_W, F), jnp.float32) * CONV_W ** -0.5,
        'ffn_conv_b': nrm(ks[17], (DEPTH, F), jnp.float32) * 0.02,
        'ffn_w_down': nrm(ks[18], (DEPTH, F, D), jnp.float32) * F ** -0.5,
        'final_g': 1.0 + 0.02 * nrm(ks[19], (D,), jnp.float32),
    }


def reference(x_prompt, x_sample, c_prompt, c_sample, ada_w, ada_b, norm1_g, norm2_g,
              na_w_qkv, na_b_qkv, na_rpb, na_w_o, sc_w_in, sc_conv_w, sc_w_out,
              ffn_w_up, ffn_conv_w, ffn_conv_b, ffn_w_down, final_g):
    p = dict(ada_w=ada_w, ada_b=ada_b, norm1_g=norm1_g, norm2_g=norm2_g,
             na_w_qkv=na_w_qkv, na_b_qkv=na_b_qkv, na_rpb=na_rpb, na_w_o=na_w_o,
             sc_w_in=sc_w_in, sc_conv_w=sc_conv_w, sc_w_out=sc_w_out,
             ffn_w_up=ffn_w_up, ffn_conv_w=ffn_conv_w, ffn_conv_b=ffn_conv_b,
             ffn_w_down=ffn_w_down, final_g=final_g)
    y_prompt = trunk(x_prompt, c_prompt, p)
    y_sample = trunk(x_sample, c_sample, p)
    return (y_prompt, y_sample)
```

```python
import functools

import numpy as np
import jax
import jax.numpy as jnp
from jax import lax
from jax.experimental import pallas as pl
from jax.experimental.pallas import tpu as pltpu

F32 = jnp.float32
BF16 = jnp.bfloat16

D_MODEL = 1024
N_HEADS = 16
HEAD_DIM = D_MODEL // N_HEADS
GRID_W = 64
WIN_H = 8
WIN_W = 16
RPB_ROWS = 2 * WIN_H - 1
D_FF = 2816
EPS = 1e-6
NEG_INF = -1e30

V7X_MXU_DIM = 256
V7X_VMEM_LIMIT_BYTES = 56 * 1024 * 1024

HEADS_PER_GROUP = V7X_MXU_DIM // HEAD_DIM
N_HEAD_GROUPS = N_HEADS // HEADS_PER_GROUP
Q_ROWS = 4
Q_TOK = Q_ROWS * GRID_W
WIN_TILES = 3 * Q_ROWS

TOKEN_TILE = 512
HALO = 16
FFN_CHUNKS = ((0, 1536), (1536, D_FF))


def _dot(a, b):
    return jnp.dot(a, b, preferred_element_type=F32)


def _silu(a):
    return a / (1.0 + jnp.exp(-a))


def _rmsnorm(x, g):
    ms = jnp.mean(x * x, axis=-1, keepdims=True)
    return x * lax.rsqrt(ms + EPS) * g


def _modnorm(x, g, scale, shift):
    return _rmsnorm(x, g) * (1.0 + scale) + shift


def _seq_valid_rows(n_rows, halo, tile, tile_idx, n_tiles):
    row = lax.broadcasted_iota(jnp.int32, (n_rows, 1), 0)
    lo_ok = jnp.where(tile_idx > 0, 0, halo)
    hi_ok = jnp.where(tile_idx < n_tiles - 1, n_rows, halo + tile)
    return (row >= lo_ok) & (row < hi_ok)


def _mod_kernel(c_ref, w_ref, b_ref, o_ref):
    s = _silu(c_ref[...])
    s_hi = s.astype(BF16)
    s_lo = (s - s_hi.astype(F32)).astype(BF16)
    w = w_ref[0]
    w_hi = w.astype(BF16)
    w_lo = (w - w_hi.astype(F32)).astype(BF16)
    acc = _dot(s_hi, w_hi) + _dot(s_lo, w_hi) + _dot(s_hi, w_lo)
    o_ref[0] = acc + b_ref[0]


def _modulation(c_all, ada_w, ada_b):
    depth = ada_w.shape[0]
    n_seq = c_all.shape[0]
    cn = D_MODEL
    return pl.pallas_call(
        _mod_kernel,
        out_shape=jax.ShapeDtypeStruct((depth, n_seq, 6 * D_MODEL), F32),
        grid=(depth, 6 * D_MODEL // cn),
        in_specs=[
            pl.BlockSpec((n_seq, D_MODEL), lambda l, n: (0, 0)),
            pl.BlockSpec((1, D_MODEL, cn), lambda l, n: (l, 0, n)),
            pl.BlockSpec((1, 1, cn), lambda l, n: (l, 0, n)),
        ],
        out_specs=pl.BlockSpec((1, n_seq, cn), lambda l, n: (l, 0, n)),
        compiler_params=pltpu.CompilerParams(
            dimension_semantics=("parallel", "parallel"),
            vmem_limit_bytes=V7X_VMEM_LIMIT_BYTES),
        name="adaln_modulation",
    )(c_all, ada_w, ada_b.reshape(depth, 1, 6 * D_MODEL))


def _qkv_kernel(x_ref, mod_ref, g_ref, w_ref, b_ref, q_ref, k_ref, v_ref):
    mod = mod_ref[0]
    h = _modnorm(x_ref[0], g_ref[...], mod[1:2], mod[0:1]).astype(BF16)
    d = D_MODEL
    q = (_dot(h, w_ref[:, 0:d]) + b_ref[:, 0:d]) * (HEAD_DIM ** -0.5)
    q_ref[0] = q.astype(BF16)
    k_ref[0] = (_dot(h, w_ref[:, d:2 * d]) + b_ref[:, d:2 * d]).astype(BF16)
    v_ref[0] = (_dot(h, w_ref[:, 2 * d:]) + b_ref[:, 2 * d:]).astype(BF16)


def _resident(shape):
    zeros = (0,) * len(shape)
    return pl.BlockSpec(shape, lambda *_: zeros, pipeline_mode=pl.Buffered(1))


def _qkv_proj(x, mod, g, w_qkv, b_qkv):
    bsz, t, d = x.shape
    tm = TOKEN_TILE
    tok = pl.BlockSpec((1, tm, d), lambda b, i: (b, i, 0))
    out = jax.ShapeDtypeStruct((bsz, t, d), BF16)
    return pl.pallas_call(
        _qkv_kernel,
        out_shape=(out, out, out),
        grid=(bsz, t // tm),
        in_specs=[
            tok,
            pl.BlockSpec((1, 6, d), lambda b, i: (b, 0, 0)),
            _resident((1, d)),
            _resident((d, 3 * d)),
            _resident((1, 3 * d)),
        ],
        out_specs=(tok, tok, tok),
        compiler_params=pltpu.CompilerParams(
            dimension_semantics=("parallel", "parallel"),
            vmem_limit_bytes=V7X_VMEM_LIMIT_BYTES),
        name="norm_qkv",
    )(x, mod, g.reshape(1, d), w_qkv, b_qkv.reshape(1, 3 * d))


def _attn_kernel(q_ref, kp_ref, kc_ref, kn_ref, vp_ref, vc_ref, vn_ref, bias_ref,
                 o_ref, s_scr, p_scr, psum_scr, o_scr):
    u = pl.program_id(2)
    n_blocks = pl.num_programs(2)
    k_refs = (kp_ref, kc_ref, kn_ref)
    v_refs = (vp_ref, vc_ref, vn_ref)
    w = V7X_MXU_DIM
    r_id = lax.broadcasted_iota(jnp.int32, (w, w), 0) // HEAD_DIM
    c_id = lax.broadcasted_iota(jnp.int32, (w, w), 1) // HEAD_DIM
    diag = r_id == c_id
    lane = lax.broadcasted_iota(jnp.int32, (GRID_W, 2 * HEAD_DIM), 1)
    low_head = lane < HEAD_DIM

    def tile(refs, j):
        blk, row = divmod(j, Q_ROWS)
        t = refs[blk][0, row * GRID_W:(row + 1) * GRID_W, :]
        t4 = jnp.concatenate([t] * HEADS_PER_GROUP, axis=0)
        return jnp.where(diag, t4, jnp.zeros_like(t4))

    def head_max(m1):
        cols = []
        for c in range(w // (2 * HEAD_DIM)):
            x = m1[:, c * 2 * HEAD_DIM:(c + 1) * 2 * HEAD_DIM]
            lo = jnp.max(jnp.where(low_head, x, NEG_INF), axis=-1, keepdims=True)
            hi = jnp.max(jnp.where(low_head, NEG_INF, x), axis=-1, keepdims=True)
            cols.append(jnp.where(low_head, lo, hi))
        return jnp.concatenate(cols, axis=-1)

    def attend(j0):
        users = []
        for j in range(WIN_TILES):
            rows = [i for i in range(Q_ROWS) if j0[i] <= j < j0[i] + WIN_H]
            if rows:
                users.append((j, rows[0], rows[-1] + 1))
        for j, ilo, ihi in users:
            kbd = tile(k_refs, j)
            s = lax.dot_general(q_ref[0, ilo * GRID_W:ihi * GRID_W, :], kbd,
                                (((1,), (1,)), ((), ())), preferred_element_type=F32)
            for i in range(ilo, ihi):
                s_scr[i, j - j0[i]] = (s[(i - ilo) * GRID_W:(i - ilo + 1) * GRID_W]
                                       + bias_ref[0, j + WIN_H // 2 - 1 - i])
        for i in range(Q_ROWS):
            m1 = s_scr[i, 0]
            for jj in range(1, WIN_H):
                m1 = jnp.maximum(m1, s_scr[i, jj])
            m = head_max(m1)
            psum = jnp.zeros((GRID_W, w), F32)
            for jj in range(WIN_H):
                p = jnp.exp(s_scr[i, jj] - m)
                psum = psum + p
                p_scr[j0[i] + jj, i * GRID_W:(i + 1) * GRID_W, :] = p.astype(BF16)
            psum_scr[i * GRID_W:(i + 1) * GRID_W, :] = psum
        for j, ilo, ihi in users:
            vbd = tile(v_refs, j)
            pv = _dot(p_scr[j, ilo * GRID_W:ihi * GRID_W, :], vbd)
            for i in range(ilo, ihi):
                sl = slice(i * GRID_W, (i + 1) * GRID_W)
                part = pv[(i - ilo) * GRID_W:(i - ilo + 1) * GRID_W]
                if j == j0[i]:
                    o_scr[sl, :] = part
                else:
                    o_scr[sl, :] = o_scr[sl, :] + part
        ones_bd = jnp.where(diag, 1.0, 0.0).astype(BF16)
        ps = psum_scr[...]
        ps_hi = ps.astype(BF16)
        ps_lo = (ps - ps_hi.astype(F32)).astype(BF16)
        denom = _dot(ps_hi, ones_bd) + _dot(ps_lo, ones_bd)
        o_ref[0] = (o_scr[...] / denom).astype(BF16)

    @pl.when(u == 0)
    def _():
        attend([Q_ROWS] * Q_ROWS)

    @pl.when(u == n_blocks - 1)
    def _():
        attend([0] * Q_ROWS)

    @pl.when((u > 0) & (u < n_blocks - 1))
    def _():
        attend(list(range(Q_ROWS)))


def _attn_bias_table(rpb):
    qc = np.arange(GRID_W)[:, None]
    kc = np.arange(GRID_W)[None, :]
    start = np.clip(qc - WIN_W // 2, 0, GRID_W - WIN_W)
    valid = (kc >= start) & (kc < start + WIN_W)
    dc = np.clip(kc - qc + WIN_W - 1, 0, 2 * WIN_W - 2)
    b = rpb.astype(F32)[:, :, dc]
    b = jnp.where(jnp.asarray(valid)[None, None], b, NEG_INF)
    b = b.reshape(N_HEAD_GROUPS, HEADS_PER_GROUP, RPB_ROWS, GRID_W, GRID_W)
    b = b.transpose(0, 2, 3, 1, 4)
    return b.reshape(N_HEAD_GROUPS, RPB_ROWS, GRID_W, HEADS_PER_GROUP * GRID_W)


def _attention(q, k, v, bias):
    bsz, t, d = q.shape
    assert t % Q_TOK == 0 and t // GRID_W >= 2 * WIN_H
    nb = t // Q_TOK
    w = V7X_MXU_DIM
    blk = (1, Q_TOK, w)
    cur = pl.BlockSpec(blk, lambda g, b, u: (b, u, g))
    prev = pl.BlockSpec(blk, lambda g, b, u: (b, jnp.maximum(u - 1, 0), g))
    nxt = pl.BlockSpec(blk, lambda g, b, u: (b, jnp.minimum(u + 1, nb - 1), g))
    return pl.pallas_call(
        _attn_kernel,
        out_shape=jax.ShapeDtypeStruct((bsz, t, d), BF16),
        grid=(N_HEAD_GROUPS, bsz, nb),
        in_specs=[cur, prev, cur, nxt, prev, cur, nxt,
                  pl.BlockSpec((1, RPB_ROWS, GRID_W, w), lambda g, b, u: (g, 0, 0, 0))],
        out_specs=cur,
        scratch_shapes=[
            pltpu.VMEM((Q_ROWS, WIN_H, GRID_W, w), F32),
            pltpu.VMEM((WIN_TILES, Q_TOK, w), BF16),
            pltpu.VMEM((Q_TOK, w), F32),
            pltpu.VMEM((Q_TOK, w), F32),
        ],
        compiler_params=pltpu.CompilerParams(
            dimension_semantics=("parallel", "parallel", "parallel"),
            vmem_limit_bytes=V7X_VMEM_LIMIT_BYTES),
        name="neighbourhood_attention",
    )(q, k, k, k, v, v, v, bias)


def _conv_ffn(x1, halo, tile_idx, n_tiles, mod, g, wup_ref, cw_ref, cb_ref, wdn_ref):
    tm = TOKEN_TILE
    n_rows = x1.shape[0]
    h2 = _modnorm(x1, g, mod[4:5], mod[3:4])
    h2 = jnp.where(_seq_valid_rows(n_rows, halo, tm, tile_idx, n_tiles), h2, 0.0)
    h2_ext = h2.astype(BF16)
    h2_main = h2[halo:halo + tm].astype(BF16)
    acc = jnp.zeros((tm, D_MODEL), F32)
    for c0, c1 in FFN_CHUNKS:
        a = _dot(h2_ext, wup_ref[:, c0:c1])
        val = _dot(h2_main, wup_ref[:, D_FF + c0:D_FF + c1])
        cw = cw_ref[:, c0:c1]
        a = (a[halo - 1:halo - 1 + tm] * cw[0:1] + a[halo:halo + tm] * cw[1:2]
             + a[halo + 1:halo + 1 + tm] * cw[2:3] + cb_ref[:, c0:c1])
        gated = (_silu(a) * val).astype(BF16)
        acc = acc + _dot(gated, wdn_ref[c0:c1, :])
    return acc


def _halo_specs(t, d):
    tm = TOKEN_TILE
    per_tile = tm // HALO
    n_halo_blocks = t // HALO
    main = pl.BlockSpec((1, tm, d), lambda b, i: (b, i, 0))
    prev = pl.BlockSpec((1, HALO, d), lambda b, i: (b, jnp.maximum(i * per_tile - 1, 0), 0))
    nxt = pl.BlockSpec(
        (1, HALO, d), lambda b, i: (b, jnp.minimum((i + 1) * per_tile, n_halo_blocks - 1), 0))
    return main, prev, nxt


def _attn_out_ffn_kernel(xm_ref, xp_ref, xn_ref, om_ref, op_ref, on_ref, mod_ref, g_ref,
                         wo_ref, wup_ref, cw_ref, cb_ref, wdn_ref, out_ref):
    tile_idx = pl.program_id(1)
    n_tiles = pl.num_programs(1)
    mod = mod_ref[0]
    x = jnp.concatenate([xp_ref[0], xm_ref[0], xn_ref[0]], axis=0)
    o = jnp.concatenate([op_ref[0], om_ref[0], on_ref[0]], axis=0)
    x1 = x + mod[2:3] * _dot(o, wo_ref[...])
    ffn = _conv_ffn(x1, HALO, tile_idx, n_tiles, mod, g_ref[...],
                    wup_ref, cw_ref, cb_ref, wdn_ref)
    out_ref[0] = x1[HALO:HALO + TOKEN_TILE] + mod[5:6] * ffn


def _attn_out_ffn(x, o, mod, g2, w_o, w_up, conv_w, conv_b, w_down):
    bsz, t, d = x.shape
    main, prev, nxt = _halo_specs(t, d)
    return pl.pallas_call(
        _attn_out_ffn_kernel,
        out_shape=jax.ShapeDtypeStruct((bsz, t, d), F32),
        grid=(bsz, t // TOKEN_TILE),
        in_specs=[
            main, prev, nxt, main, prev, nxt,
            pl.BlockSpec((1, 6, d), lambda b, i: (b, 0, 0)),
            _resident((1, d)),
            _resident((d, d)),
            _resident((d, 2 * D_FF)),
            _resident((3, D_FF)),
            _resident((1, D_FF)),
            _resident((D_FF, d)),
        ],
        out_specs=main,
        compiler_params=pltpu.CompilerParams(
            dimension_semantics=("parallel", "parallel"),
            vmem_limit_bytes=V7X_VMEM_LIMIT_BYTES),
        name="attn_out_conv_ffn",
    )(x, x, x, o, o, o, mod, g2.reshape(1, d), w_o, w_up, conv_w,
      conv_b.reshape(1, D_FF), w_down)


def _shortconv_layer_kernel(xm_ref, xp_ref, xn_ref, mod_ref, g1_ref, g2_ref, gf_ref,
                            win_ref, scw_ref, wout_ref, wup_ref, cw_ref, cb_ref, wdn_ref,
                            out_ref):
    tile_idx = pl.program_id(1)
    n_tiles = pl.num_programs(1)
    tm = TOKEN_TILE
    d = D_MODEL
    inner = HALO // 2
    n1 = tm + 2 * inner
    mod = mod_ref[0]
    x = jnp.concatenate([xp_ref[0], xm_ref[0], xn_ref[0]], axis=0)
    h1 = _modnorm(x, g1_ref[...], mod[1:2], mod[0:1]).astype(BF16)
    cg = _dot(h1, win_ref[:, d:2 * d])
    uu = _dot(h1, win_ref[:, 2 * d:])
    valid = _seq_valid_rows(tm + 2 * HALO, HALO, tm, tile_idx, n_tiles)
    cu = jnp.where(valid, cg * uu, 0.0)
    scw = scw_ref[...]
    lo = HALO - inner
    y = (cu[lo - 1:lo - 1 + n1] * scw[0:1] + cu[lo:lo + n1] * scw[1:2]
         + cu[lo + 1:lo + 1 + n1] * scw[2:3])
    bg = _dot(h1, win_ref[:, 0:d])
    z = (bg[lo:lo + n1] * y).astype(BF16)
    x1 = x[lo:lo + n1] + mod[2:3] * _dot(z, wout_ref[...])
    ffn = _conv_ffn(x1, inner, tile_idx, n_tiles, mod, g2_ref[...],
                    wup_ref, cw_ref, cb_ref, wdn_ref)
    x2 = x1[inner:inner + tm] + mod[5:6] * ffn
    out_ref[0] = _rmsnorm(x2, gf_ref[...])


def _shortconv_layer(x, mod, g1, g2, gf, w_in, sc_conv_w, w_out, w_up, conv_w, conv_b, w_down):
    bsz, t, d = x.shape
    main, prev, nxt = _halo_specs(t, d)
    return pl.pallas_call(
        _shortconv_layer_kernel,
        out_shape=jax.ShapeDtypeStruct((bsz, t, d), F32),
        grid=(bsz, t // TOKEN_TILE),
        in_specs=[
            main, prev, nxt,
            pl.BlockSpec((1, 6, d), lambda b, i: (b, 0, 0)),
            _resident((1, d)), _resident((1, d)), _resident((1, d)),
            _resident((d, 3 * d)),
            _resident((3, d)),
            _resident((d, d)),
            _resident((d, 2 * D_FF)),
            _resident((3, D_FF)),
            _resident((1, D_FF)),
            _resident((D_FF, d)),
        ],
        out_specs=main,
        compiler_params=pltpu.CompilerParams(
            dimension_semantics=("parallel", "parallel"),
            vmem_limit_bytes=V7X_VMEM_LIMIT_BYTES),
        name="shortconv_layer",
    )(x, x, x, mod, g1.reshape(1, d), g2.reshape(1, d), gf.reshape(1, d), w_in, sc_conv_w,
      w_out, w_up, conv_w, conv_b.reshape(1, D_FF), w_down)


def _trunk(x, mod, p):
    assert x.shape[1] % TOKEN_TILE == 0
    q, k, v = _qkv_proj(x, mod[0], p["norm1_g"][0], p["w_qkv"], p["b_qkv"])
    o = _attention(q, k, v, p["attn_bias"])
    x = _attn_out_ffn(x, o, mod[0], p["norm2_g"][0], p["w_o"], p["w_up"][0],
                      p["ffn_conv_w"][0], p["ffn_conv_b"][0], p["w_down"][0])
    return _shortconv_layer(x, mod[1], p["norm1_g"][1], p["norm2_g"][1], p["final_g"],
                            p["w_in"], p["sc_conv_w"], p["w_out"], p["w_up"][1],
                            p["ffn_conv_w"][1], p["ffn_conv_b"][1], p["w_down"][1])


def kernel(x_prompt, x_sample, c_prompt, c_sample, ada_w, ada_b, norm1_g, norm2_g,
           na_w_qkv, na_b_qkv, na_rpb, na_w_o, sc_w_in, sc_conv_w, sc_w_out,
           ffn_w_up, ffn_conv_w, ffn_conv_b, ffn_w_down, final_g):
    n_prompt = c_prompt.shape[0]
    n_sample = c_sample.shape[0]
    n_seq = n_prompt + n_sample
    n_pad = -n_seq % 8
    c_all = jnp.concatenate([c_prompt, c_sample, jnp.zeros((n_pad, D_MODEL), F32)], axis=0)
    mod = _modulation(c_all, ada_w, ada_b)
    mod = mod.reshape(mod.shape[0], n_seq + n_pad, 6, D_MODEL)
    p = dict(
        norm1_g=norm1_g, norm2_g=norm2_g, final_g=final_g,
        w_qkv=na_w_qkv[0].astype(BF16), b_qkv=na_b_qkv[0],
        attn_bias=_attn_bias_table(na_rpb[0]),
        w_o=na_w_o[0].astype(BF16),
        w_in=sc_w_in[0].astype(BF16), sc_conv_w=sc_conv_w[0], w_out=sc_w_out[0].astype(BF16),
        w_up=ffn_w_up.astype(BF16), ffn_conv_w=ffn_conv_w, ffn_conv_b=ffn_conv_b,
        w_down=ffn_w_down.astype(BF16),
    )
    y_prompt = _trunk(x_prompt, mod[:, :n_prompt], p)
    y_sample = _trunk(x_sample, mod[:, n_prompt:n_seq], p)
    return (y_prompt, y_sample)
```

```python
import functools

import numpy as np
import jax
import jax.numpy as jnp
from jax import lax
from jax.experimental import pallas as pl
from jax.experimental.pallas import tpu as pltpu

F32 = jnp.float32
BF16 = jnp.bfloat16

D_MODEL = 1024
N_HEADS = 16
HEAD_DIM = D_MODEL // N_HEADS
GRID_W = 64
WIN_H = 8
WIN_W = 16
RPB_ROWS = 2 * WIN_H - 1
D_FF = 2816
EPS = 1e-6
NEG_INF = -1e30

V7X_MXU_DIM = 256
V7X_VMEM_LIMIT_BYTES = 56 * 1024 * 1024

HEADS_PER_GROUP = V7X_MXU_DIM // HEAD_DIM
N_HEAD_GROUPS = N_HEADS // HEADS_PER_GROUP
Q_ROWS = 8
Q_TOK = Q_ROWS * GRID_W
KEY_BLOCK_ROWS = 4
KEY_BLOCK_TOK = KEY_BLOCK_ROWS * GRID_W
N_KEY_SLOTS = (Q_ROWS + WIN_H) // KEY_BLOCK_ROWS
WIN_TILES = Q_ROWS + WIN_H - 1
N_ATTN_SLOTS = 2

TOKEN_TILE = 512
HALO = 16
FFN_CHUNKS = ((0, 1536), (1536, D_FF))


def _dot(a, b):
    return jnp.dot(a, b, preferred_element_type=F32)


def _silu(a):
    return a / (1.0 + jnp.exp(-a))


def _rmsnorm(x, g):
    ms = jnp.mean(x * x, axis=-1, keepdims=True)
    return x * lax.rsqrt(ms + EPS) * g


def _modnorm(x, g, scale, shift):
    return _rmsnorm(x, g) * (1.0 + scale) + shift


def _seq_valid_rows(n_rows, halo, tile, tile_idx, n_tiles):
    row = lax.broadcasted_iota(jnp.int32, (n_rows, 1), 0)
    lo_ok = jnp.where(tile_idx > 0, 0, halo)
    hi_ok = jnp.where(tile_idx < n_tiles - 1, n_rows, halo + tile)
    return (row >= lo_ok) & (row < hi_ok)


def _mod_kernel(c_ref, w_ref, b_ref, o_ref):
    s = _silu(c_ref[...])
    s_hi = s.astype(BF16)
    s_lo = (s - s_hi.astype(F32)).astype(BF16)
    w = w_ref[0]
    w_hi = w.astype(BF16)
    w_lo = (w - w_hi.astype(F32)).astype(BF16)
    acc = _dot(s_hi, w_hi) + _dot(s_lo, w_hi) + _dot(s_hi, w_lo)
    o_ref[0] = acc + b_ref[0]


def _modulation(c_all, ada_w, ada_b):
    depth = ada_w.shape[0]
    n_seq = c_all.shape[0]
    cn = D_MODEL
    return pl.pallas_call(
        _mod_kernel,
        out_shape=jax.ShapeDtypeStruct((depth, n_seq, 6 * D_MODEL), F32),
        grid=(depth, 6 * D_MODEL // cn),
        in_specs=[
            pl.BlockSpec((n_seq, D_MODEL), lambda l, n: (0, 0)),
            pl.BlockSpec((1, D_MODEL, cn), lambda l, n: (l, 0, n)),
            pl.BlockSpec((1, 1, cn), lambda l, n: (l, 0, n)),
        ],
        out_specs=pl.BlockSpec((1, n_seq, cn), lambda l, n: (l, 0, n)),
        compiler_params=pltpu.CompilerParams(
            dimension_semantics=("parallel", "parallel"),
            vmem_limit_bytes=V7X_VMEM_LIMIT_BYTES),
        name="adaln_modulation",
    )(c_all, ada_w, ada_b.reshape(depth, 1, 6 * D_MODEL))


def _qkv_kernel(x_ref, mod_ref, g_ref, w_ref, b_ref, q_ref, k_ref, v_ref):
    mod = mod_ref[0]
    h = _modnorm(x_ref[0], g_ref[...], mod[1:2], mod[0:1]).astype(BF16)
    d = D_MODEL
    q = (_dot(h, w_ref[:, 0:d]) + b_ref[:, 0:d]) * (HEAD_DIM ** -0.5)
    q_ref[0] = q.astype(BF16)
    k_ref[0] = (_dot(h, w_ref[:, d:2 * d]) + b_ref[:, d:2 * d]).astype(BF16)
    v_ref[0] = (_dot(h, w_ref[:, 2 * d:]) + b_ref[:, 2 * d:]).astype(BF16)


def _resident(shape):
    zeros = (0,) * len(shape)
    return pl.BlockSpec(shape, lambda *_: zeros, pipeline_mode=pl.Buffered(1))


def _qkv_proj(x, mod, g, w_qkv, b_qkv):
    bsz, t, d = x.shape
    tm = TOKEN_TILE
    tok = pl.BlockSpec((1, tm, d), lambda b, i: (b, i, 0))
    out = jax.ShapeDtypeStruct((bsz, t, d), BF16)
    return pl.pallas_call(
        _qkv_kernel,
        out_shape=(out, out, out),
        grid=(bsz, t // tm),
        in_specs=[
            tok,
            pl.BlockSpec((1, 6, d), lambda b, i: (b, 0, 0)),
            _resident((1, d)),
            _resident((d, 3 * d)),
            _resident((1, 3 * d)),
        ],
        out_specs=(tok, tok, tok),
        compiler_params=pltpu.CompilerParams(
            dimension_semantics=("parallel", "parallel"),
            vmem_limit_bytes=V7X_VMEM_LIMIT_BYTES),
        name="norm_qkv",
    )(x, mod, g.reshape(1, d), w_qkv, b_qkv.reshape(1, 3 * d))


def _key_slot_blocks(u, n_steps):
    first = u == 0
    last = u == n_steps - 1
    return (jnp.where(first, 1, 2 * u - 1),
            2 * u,
            2 * u + 1,
            jnp.where(first, 2, jnp.where(last, 2 * u, 2 * u + 2)))


def _attn_kernel(q_ref, k0_ref, k1_ref, k2_ref, k3_ref, v0_ref, v1_ref, v2_ref, v3_ref,
                 bias_ref, o_ref, s_scr, p_scr, psum_scr, o_scr):
    u = pl.program_id(1)
    k_refs = (k0_ref, k1_ref, k2_ref, k3_ref)
    v_refs = (v0_ref, v1_ref, v2_ref, v3_ref)
    w = V7X_MXU_DIM
    half = 2 * HEAD_DIM
    slot_off = [KEY_BLOCK_ROWS * blk - Q_ROWS * u + WIN_H - 1
                for blk in _key_slot_blocks(u, pl.num_programs(1))]
    lane = lax.broadcasted_iota(jnp.int32, (GRID_W, half), 1)
    low_head = lane < HEAD_DIM
    keep_low = jnp.where(low_head, 1.0, 0.0).astype(BF16)
    keep_high = jnp.where(low_head, 0.0, 1.0).astype(BF16)
    zero_half = jnp.zeros((GRID_W, half), BF16)
    r_id = lax.broadcasted_iota(jnp.int32, (w, w), 0) // HEAD_DIM
    c_id = lax.broadcasted_iota(jnp.int32, (w, w), 1) // HEAD_DIM
    ones_bd = jnp.where(r_id == c_id, 1.0, 0.0).astype(BF16)
    users = [(j, max(0, j - WIN_H + 1), min(Q_ROWS, j + 1)) for j in range(WIN_TILES)]

    def tile(refs, j, lanes):
        slot, row = divmod(j, KEY_BLOCK_ROWS)
        t = refs[slot][0, row * GRID_W:(row + 1) * GRID_W, lanes]
        t01, t23 = t[:, :half], t[:, half:]
        return jnp.concatenate([
            jnp.concatenate([t01 * keep_low, zero_half], axis=1),
            jnp.concatenate([t01 * keep_high, zero_half], axis=1),
            jnp.concatenate([zero_half, t23 * keep_low], axis=1),
            jnp.concatenate([zero_half, t23 * keep_high], axis=1)], axis=0)

    def head_max(m1):
        cols = []
        for c in range(w // half):
            x = m1[:, c * half:(c + 1) * half]
            lo = jnp.max(jnp.where(low_head, x, NEG_INF), axis=-1, keepdims=True)
            hi = jnp.max(jnp.where(low_head, NEG_INF, x), axis=-1, keepdims=True)
            cols.append(jnp.where(low_head, lo, hi))
        return jnp.concatenate(cols, axis=-1)

    for g in range(N_HEAD_GROUPS):
        sl = g % N_ATTN_SLOTS
        lanes = slice(g * w, (g + 1) * w)
        for j, ilo, ihi in users:
            kbd = tile(k_refs, j, lanes)
            s = lax.dot_general(q_ref[0, ilo * GRID_W:ihi * GRID_W, lanes], kbd,
                                (((1,), (1,)), ((), ())), preferred_element_type=F32)
            for i in range(ilo, ihi):
                d = slot_off[j // KEY_BLOCK_ROWS] + (j % KEY_BLOCK_ROWS - i)
                s_scr[sl, i, j - i] = (s[(i - ilo) * GRID_W:(i - ilo + 1) * GRID_W]
                                       + bias_ref[g, d])
        for i in range(Q_ROWS):
            m1 = s_scr[sl, i, 0]
            for jj in range(1, WIN_H):
                m1 = jnp.maximum(m1, s_scr[sl, i, jj])
            m = head_max(m1)
            psum = jnp.zeros((GRID_W, w), F32)
            for jj in range(WIN_H):
                p = jnp.exp(s_scr[sl, i, jj] - m)
                psum = psum + p
                p_scr[sl, i + jj, i * GRID_W:(i + 1) * GRID_W, :] = p.astype(BF16)
            psum_scr[sl, i * GRID_W:(i + 1) * GRID_W, :] = psum
        for j, ilo, ihi in users:
            vbd = tile(v_refs, j, lanes)
            pv = _dot(p_scr[sl, j, ilo * GRID_W:ihi * GRID_W, :], vbd)
            for i in range(ilo, ihi):
                rows = slice(i * GRID_W, (i + 1) * GRID_W)
                part = pv[(i - ilo) * GRID_W:(i - ilo + 1) * GRID_W]
                if j == i:
                    o_scr[sl, rows, :] = part
                else:
                    o_scr[sl, rows, :] = o_scr[sl, rows, :] + part
        ps = psum_scr[sl]
        ps_hi = ps.astype(BF16)
        ps_lo = (ps - ps_hi.astype(F32)).astype(BF16)
        denom = _dot(ps_hi, ones_bd) + _dot(ps_lo, ones_bd)
        o_ref[0, :, lanes] = (o_scr[sl] / denom).astype(BF16)


def _attn_bias_table(rpb):
    qc = np.arange(GRID_W)[:, None]
    kc = np.arange(GRID_W)[None, :]
    start = np.clip(qc - WIN_W // 2, 0, GRID_W - WIN_W)
    valid = (kc >= start) & (kc < start + WIN_W)
    dc = np.clip(kc - qc + WIN_W - 1, 0, 2 * WIN_W - 2)
    b = rpb.astype(F32)[:, :, dc]
    b = jnp.where(jnp.asarray(valid)[None, None], b, NEG_INF)
    b = b.reshape(N_HEAD_GROUPS, HEADS_PER_GROUP, RPB_ROWS, GRID_W, GRID_W)
    b = b.transpose(0, 2, 3, 1, 4)
    return b.reshape(N_HEAD_GROUPS, RPB_ROWS, GRID_W, HEADS_PER_GROUP * GRID_W)


def _attention(q, k, v, bias):
    bsz, t, d = q.shape
    assert t % Q_TOK == 0 and t // Q_TOK >= 2
    n_steps = t // Q_TOK
    w = V7X_MXU_DIM
    tok = pl.BlockSpec((1, Q_TOK, d), lambda b, u: (b, u, 0))

    def key_slot(slot):
        return pl.BlockSpec((1, KEY_BLOCK_TOK, d),
                            lambda b, u: (b, _key_slot_blocks(u, n_steps)[slot], 0))

    key_slots = [key_slot(s) for s in range(N_KEY_SLOTS)]
    return pl.pallas_call(
        _attn_kernel,
        out_shape=jax.ShapeDtypeStruct((bsz, t, d), BF16),
        grid=(bsz, n_steps),
        in_specs=[tok] + key_slots + key_slots
                 + [_resident((N_HEAD_GROUPS, RPB_ROWS, GRID_W, w))],
        out_specs=tok,
        scratch_shapes=[
            pltpu.VMEM((N_ATTN_SLOTS, Q_ROWS, WIN_H, GRID_W, w), F32),
            pltpu.VMEM((N_ATTN_SLOTS, WIN_TILES, Q_TOK, w), BF16),
            pltpu.VMEM((N_ATTN_SLOTS, Q_TOK, w), F32),
            pltpu.VMEM((N_ATTN_SLOTS, Q_TOK, w), F32),
        ],
        compiler_params=pltpu.CompilerParams(
            dimension_semantics=("parallel", "parallel"),
            vmem_limit_bytes=V7X_VMEM_LIMIT_BYTES),
        name="neighbourhood_attention",
    )(q, *([k] * N_KEY_SLOTS), *([v] * N_KEY_SLOTS), bias)


def _conv_ffn(x1, halo, tile_idx, n_tiles, mod, g, wup_ref, cw_ref, cb_ref, wdn_ref):
    tm = TOKEN_TILE
    n_rows = x1.shape[0]
    h2 = _modnorm(x1, g, mod[4:5], mod[3:4])
    h2 = jnp.where(_seq_valid_rows(n_rows, halo, tm, tile_idx, n_tiles), h2, 0.0)
    h2_ext = h2.astype(BF16)
    h2_main = h2[halo:halo + tm].astype(BF16)
    acc = jnp.zeros((tm, D_MODEL), F32)
    for c0, c1 in FFN_CHUNKS:
        a = _dot(h2_ext, wup_ref[:, c0:c1])
        val = _dot(h2_main, wup_ref[:, D_FF + c0:D_FF + c1])
        cw = cw_ref[:, c0:c1]
        a = (a[halo - 1:halo - 1 + tm] * cw[0:1] + a[halo:halo + tm] * cw[1:2]
             + a[halo + 1:halo + 1 + tm] * cw[2:3] + cb_ref[:, c0:c1])
        gated = (_silu(a) * val).astype(BF16)
        acc = acc + _dot(gated, wdn_ref[c0:c1, :])
    return acc


def _halo_specs(t, d):
    tm = TOKEN_TILE
    per_tile = tm // HALO
    n_halo_blocks = t // HALO
    main = pl.BlockSpec((1, tm, d), lambda b, i: (b, i, 0))
    prev = pl.BlockSpec((1, HALO, d), lambda b, i: (b, jnp.maximum(i * per_tile - 1, 0), 0))
    nxt = pl.BlockSpec(
        (1, HALO, d), lambda b, i: (b, jnp.minimum((i + 1) * per_tile, n_halo_blocks - 1), 0))
    return main, prev, nxt


def _attn_out_ffn_kernel(xm_ref, xp_ref, xn_ref, om_ref, op_ref, on_ref, mod_ref, g_ref,
                         wo_ref, wup_ref, cw_ref, cb_ref, wdn_ref, out_ref):
    tile_idx = pl.program_id(1)
    n_tiles = pl.num_programs(1)
    mod = mod_ref[0]
    x = jnp.concatenate([xp_ref[0], xm_ref[0], xn_ref[0]], axis=0)
    o = jnp.concatenate([op_ref[0], om_ref[0], on_ref[0]], axis=0)
    x1 = x + mod[2:3] * _dot(o, wo_ref[...])
    ffn = _conv_ffn(x1, HALO, tile_idx, n_tiles, mod, g_ref[...],
                    wup_ref, cw_ref, cb_ref, wdn_ref)
    out_ref[0] = x1[HALO:HALO + TOKEN_TILE] + mod[5:6] * ffn


def _attn_out_ffn(x, o, mod, g2, w_o, w_up, conv_w, conv_b, w_down):
    bsz, t, d = x.shape
    main, prev, nxt = _halo_specs(t, d)
    return pl.pallas_call(
        _attn_out_ffn_kernel,
        out_shape=jax.ShapeDtypeStruct((bsz, t, d), F32),
        grid=(bsz, t // TOKEN_TILE),
        in_specs=[
            main, prev, nxt, main, prev, nxt,
            pl.BlockSpec((1, 6, d), lambda b, i: (b, 0, 0)),
            _resident((1, d)),
            _resident((d, d)),
            _resident((d, 2 * D_FF)),
            _resident((3, D_FF)),
            _resident((1, D_FF)),
            _resident((D_FF, d)),
        ],
        out_specs=main,
        compiler_params=pltpu.CompilerParams(
            dimension_semantics=("parallel", "parallel"),
            vmem_limit_bytes=V7X_VMEM_LIMIT_BYTES),
        name="attn_out_conv_ffn",
    )(x, x, x, o, o, o, mod, g2.reshape(1, d), w_o, w_up, conv_w,
      conv_b.reshape(1, D_FF), w_down)


def _shortconv_layer_kernel(xm_ref, xp_ref, xn_ref, mod_ref, g1_ref, g2_ref, gf_ref,
                            win_ref, scw_ref, wout_ref, wup_ref, cw_ref, cb_ref, wdn_ref,
                            out_ref):
    tile_idx = pl.program_id(1)
    n_tiles = pl.num_programs(1)
    tm = TOKEN_TILE
    d = D_MODEL
    inner = HALO // 2
    n1 = tm + 2 * inner
    mod = mod_ref[0]
    x = jnp.concatenate([xp_ref[0], xm_ref[0], xn_ref[0]], axis=0)
    h1 = _modnorm(x, g1_ref[...], mod[1:2], mod[0:1]).astype(BF16)
    cg = _dot(h1, win_ref[:, d:2 * d])
    uu = _dot(h1, win_ref[:, 2 * d:])
    valid = _seq_valid_rows(tm + 2 * HALO, HALO, tm, tile_idx, n_tiles)
    cu = jnp.where(valid, cg * uu, 0.0)
    scw = scw_ref[...]
    lo = HALO - inner
    y = (cu[lo - 1:lo - 1 + n1] * scw[0:1] + cu[lo:lo + n1] * scw[1:2]
         + cu[lo + 1:lo + 1 + n1] * scw[2:3])
    bg = _dot(h1, win_ref[:, 0:d])
    z = (bg[lo:lo + n1] * y).astype(BF16)
    x1 = x[lo:lo + n1] + mod[2:3] * _dot(z, wout_ref[...])
    ffn = _conv_ffn(x1, inner, tile_idx, n_tiles, mod, g2_ref[...],
                    wup_ref, cw_ref, cb_ref, wdn_ref)
    x2 = x1[inner:inner + tm] + mod[5:6] * ffn
    out_ref[0] = _rmsnorm(x2, gf_ref[...])


def _shortconv_layer(x, mod, g1, g2, gf, w_in, sc_conv_w, w_out, w_up, conv_w, conv_b, w_down):
    bsz, t, d = x.shape
    main, prev, nxt = _halo_specs(t, d)
    return pl.pallas_call(
        _shortconv_layer_kernel,
        out_shape=jax.ShapeDtypeStruct((bsz, t, d), F32),
        grid=(bsz, t // TOKEN_TILE),
        in_specs=[
            main, prev, nxt,
            pl.BlockSpec((1, 6, d), lambda b, i: (b, 0, 0)),
            _resident((1, d)), _resident((1, d)), _resident((1, d)),
            _resident((d, 3 * d)),
            _resident((3, d)),
            _resident((d, d)),
            _resident((d, 2 * D_FF)),
            _resident((3, D_FF)),
            _resident((1, D_FF)),
            _resident((D_FF, d)),
        ],
        out_specs=main,
        compiler_params=pltpu.CompilerParams(
            dimension_semantics=("parallel", "parallel"),
            vmem_limit_bytes=V7X_VMEM_LIMIT_BYTES),
        name="shortconv_layer",
    )(x, x, x, mod, g1.reshape(1, d), g2.reshape(1, d), gf.reshape(1, d), w_in, sc_conv_w,
      w_out, w_up, conv_w, conv_b.reshape(1, D_FF), w_down)


def _trunk(x, mod, p):
    assert x.shape[1] % TOKEN_TILE == 0
    q, k, v = _qkv_proj(x, mod[0], p["norm1_g"][0], p["w_qkv"], p["b_qkv"])
    o = _attention(q, k, v, p["attn_bias"])
    x = _attn_out_ffn(x, o, mod[0], p["norm2_g"][0], p["w_o"], p["w_up"][0],
                      p["ffn_conv_w"][0], p["ffn_conv_b"][0], p["w_down"][0])
    return _shortconv_layer(x, mod[1], p["norm1_g"][1], p["norm2_g"][1], p["final_g"],
                            p["w_in"], p["sc_conv_w"], p["w_out"], p["w_up"][1],
                            p["ffn_conv_w"][1], p["ffn_conv_b"][1], p["w_down"][1])


def kernel(x_prompt, x_sample, c_prompt, c_sample, ada_w, ada_b, norm1_g, norm2_g,
           na_w_qkv, na_b_qkv, na_rpb, na_w_o, sc_w_in, sc_conv_w, sc_w_out,
           ffn_w_up, ffn_conv_w, ffn_conv_b, ffn_w_down, final_g):
    n_prompt = c_prompt.shape[0]
    n_sample = c_sample.shape[0]
    n_seq = n_prompt + n_sample
    n_pad = -n_seq % 8
    c_all = jnp.concatenate([c_prompt, c_sample, jnp.zeros((n_pad, D_MODEL), F32)], axis=0)
    mod = _modulation(c_all, ada_w, ada_b)
    mod = mod.reshape(mod.shape[0], n_seq + n_pad, 6, D_MODEL)
    p = dict(
        norm1_g=norm1_g, norm2_g=norm2_g, final_g=final_g,
        w_qkv=na_w_qkv[0].astype(BF16), b_qkv=na_b_qkv[0],
        attn_bias=_attn_bias_table(na_rpb[0]),
        w_o=na_w_o[0].astype(BF16),
        w_in=sc_w_in[0].astype(BF16), sc_conv_w=sc_conv_w[0], w_out=sc_w_out[0].astype(BF16),
        w_up=[w.astype(BF16) for w in ffn_w_up], ffn_conv_w=ffn_conv_w, ffn_conv_b=ffn_conv_b,
        w_down=[w.astype(BF16) for w in ffn_w_down],
    )
    y_prompt = _trunk(x_prompt, mod[:, :n_prompt], p)
    y_sample = _trunk(x_sample, mod[:, n_prompt:n_seq], p)
    return (y_prompt, y_sample)
```

```python
import math

import numpy as np
import jax
import jax.numpy as jnp
from jax import lax
from jax.experimental import pallas as pl
from jax.experimental.pallas import tpu as pltpu

F32 = jnp.float32
BF16 = jnp.bfloat16

D_MODEL = 1024
N_HEADS = 16
HEAD_DIM = D_MODEL // N_HEADS
GRID_W = 64
WIN_H = 8
WIN_W = 16
RPB_ROWS = 2 * WIN_H - 1
RPB_COLS = 2 * WIN_W - 1
D_FF = 2816
EPS = 1e-6
NEG_INF = -1e30
LOG2E = math.log2(math.e)

V7X_LANES = 128
V7X_MXU_DIM = 256
V7X_VMEM_LIMIT_BYTES = 56 * 1024 * 1024

HEADS_PER_GROUP = V7X_MXU_DIM // HEAD_DIM
N_HEAD_GROUPS = N_HEADS // HEADS_PER_GROUP
Q_ROWS = 8
Q_TOK = Q_ROWS * GRID_W
KEY_BLOCK_ROWS = 4
KEY_BLOCK_TOK = KEY_BLOCK_ROWS * GRID_W
N_KEY_SLOTS = (Q_ROWS + WIN_H) // KEY_BLOCK_ROWS
WIN_TILES = Q_ROWS + WIN_H - 1
N_ATTN_SLOTS = 2

TOKEN_TILE = 512
SUB_TILE = 512
N_SUB = TOKEN_TILE // SUB_TILE
HALO = 16
FFN_CHUNKS = ((0, 768), (768, 1536), (1536, 2304), (2304, D_FF))


def _dot(a, b):
    return jnp.dot(a, b, preferred_element_type=F32)


def _silu(a):
    return a / (1.0 + jnp.exp(-a))


def _rmsnorm(x, g):
    ms = jnp.mean(x * x, axis=-1, keepdims=True)
    return x * lax.rsqrt(ms + EPS) * g


def _modnorm(x, g, scale, shift):
    return _rmsnorm(x, g) * (1.0 + scale) + shift


def _zero_outside_sequence(v, halo, lo_inside, hi_inside):
    if lo_inside is True and hi_inside is True:
        return v
    n_rows = v.shape[0]
    row = lax.broadcasted_iota(jnp.int32, (n_rows, 1), 0)
    lo_ok = 0 if lo_inside is True else jnp.where(lo_inside, 0, halo)
    hi_ok = n_rows if hi_inside is True else jnp.where(hi_inside, n_rows, n_rows - halo)
    return jnp.where((row >= lo_ok) & (row < hi_ok), v, 0.0)


def _resident(shape, index=None):
    index = (0,) * len(shape) if index is None else index
    return pl.BlockSpec(shape, lambda *_: index, pipeline_mode=pl.Buffered(1))


def _layer_block(arr, layer):
    return _resident((1,) + arr.shape[1:], (layer,) + (0,) * (arr.ndim - 1))


def _params(dims):
    return pltpu.CompilerParams(dimension_semantics=("parallel",) * dims,
                                vmem_limit_bytes=V7X_VMEM_LIMIT_BYTES)


def _mod_kernel(c_ref, w_ref, b_ref, o_ref):
    s = _silu(c_ref[...])
    s_hi = s.astype(BF16)
    s_lo = (s - s_hi.astype(F32)).astype(BF16)
    w = w_ref[0]
    w_hi = w.astype(BF16)
    w_lo = (w - w_hi.astype(F32)).astype(BF16)
    acc = _dot(s_hi, w_hi) + _dot(s_lo, w_hi) + _dot(s_hi, w_lo)
    o_ref[0] = acc + b_ref[0]


def _modulation(c_all, ada_w, ada_b):
    depth = ada_w.shape[0]
    n_seq = c_all.shape[0]
    cn = D_MODEL
    return pl.pallas_call(
        _mod_kernel,
        out_shape=jax.ShapeDtypeStruct((depth, n_seq, 6 * D_MODEL), F32),
        grid=(depth, 6 * D_MODEL // cn),
        in_specs=[
            pl.BlockSpec((n_seq, D_MODEL), lambda l, n: (0, 0)),
            pl.BlockSpec((1, D_MODEL, cn), lambda l, n: (l, 0, n)),
            pl.BlockSpec((1, 1, cn), lambda l, n: (l, 0, n)),
        ],
        out_specs=pl.BlockSpec((1, n_seq, cn), lambda l, n: (l, 0, n)),
        compiler_params=_params(2),
        name="adaln_modulation",
    )(c_all, ada_w, ada_b.reshape(depth, 1, 6 * D_MODEL))


def _qkv_kernel(x_ref, mod_ref, g_ref, w_ref, b_ref, q_ref, k_ref, v_ref):
    mod = mod_ref[0, 0]
    h = _modnorm(x_ref[0], g_ref[0], mod[1:2], mod[0:1]).astype(BF16)
    d = D_MODEL
    q = (_dot(h, w_ref[0, :, 0:d]) + b_ref[0, :, 0:d]) * (HEAD_DIM ** -0.5 * LOG2E)
    q_ref[0] = q.astype(BF16)
    k_ref[0] = (_dot(h, w_ref[0, :, d:2 * d]) + b_ref[0, :, d:2 * d]).astype(BF16)
    v_ref[0] = (_dot(h, w_ref[0, :, 2 * d:]) + b_ref[0, :, 2 * d:]).astype(BF16)


def _qkv_proj(x, mod, seq0, layer, na_layer, g, w_qkv, b_qkv):
    bsz, t, d = x.shape
    tm = TOKEN_TILE
    tok = pl.BlockSpec((1, tm, d), lambda b, i: (b, i, 0))
    out = jax.ShapeDtypeStruct((bsz, t, d), BF16)
    return pl.pallas_call(
        _qkv_kernel,
        out_shape=(out, out, out),
        grid=(bsz, t // tm),
        in_specs=[
            tok,
            pl.BlockSpec((1, 1, 6, d), lambda b, i: (layer, seq0 + b, 0, 0)),
            _layer_block(g, layer),
            _layer_block(w_qkv, na_layer),
            _layer_block(b_qkv, na_layer),
        ],
        out_specs=(tok, tok, tok),
        compiler_params=_params(2),
        name="norm_qkv",
    )(x, mod, g, w_qkv, b_qkv)


def _bias_kernel(rpb_ref, o_ref):
    shape = (GRID_W, V7X_LANES)
    qc = lax.broadcasted_iota(jnp.int32, shape, 0)
    lane = lax.broadcasted_iota(jnp.int32, shape, 1)
    kc = lane & (GRID_W - 1)
    start = jnp.clip(qc - WIN_W // 2, 0, GRID_W - WIN_W)
    valid = (kc >= start) & (kc < start + WIN_W)
    low_head = lane < HEAD_DIM
    shift_low = V7X_LANES - (WIN_W - 1)
    shift_high = (shift_low + HEAD_DIM) % V7X_LANES
    for d in range(RPB_ROWS):
        halves = []
        for c in range(HEADS_PER_GROUP // 2):
            r_low = jnp.broadcast_to(rpb_ref[0, 2 * c, d:d + 1, :], shape)
            r_high = jnp.broadcast_to(rpb_ref[0, 2 * c + 1, d:d + 1, :], shape)
            t_low = pltpu.roll(r_low, shift_low, 1, stride=1, stride_axis=0)
            t_high = pltpu.roll(r_high, shift_high, 1, stride=1, stride_axis=0)
            b = jnp.where(low_head, t_low, t_high) * LOG2E
            halves.append(jnp.where(valid, b, NEG_INF))
        o_ref[0, d] = jnp.concatenate(halves, axis=1)


def _attn_bias_table(rpb):
    r = jnp.pad(rpb.astype(F32), ((0, 0), (0, 0), (0, V7X_LANES - RPB_COLS)))
    r = r.reshape(N_HEAD_GROUPS, HEADS_PER_GROUP, RPB_ROWS, V7X_LANES)
    w = HEADS_PER_GROUP * GRID_W
    return pl.pallas_call(
        _bias_kernel,
        out_shape=jax.ShapeDtypeStruct((N_HEAD_GROUPS, RPB_ROWS, GRID_W, w), F32),
        grid=(N_HEAD_GROUPS,),
        in_specs=[pl.BlockSpec((1, HEADS_PER_GROUP, RPB_ROWS, V7X_LANES),
                               lambda g: (g, 0, 0, 0))],
        out_specs=pl.BlockSpec((1, RPB_ROWS, GRID_W, w), lambda g: (g, 0, 0, 0)),
        compiler_params=_params(1),
        name="attn_bias_table",
    )(r)


def _key_slot_blocks(u, n_steps):
    first = u == 0
    last = u == n_steps - 1
    return (jnp.where(first, 1, 2 * u - 1),
            2 * u,
            2 * u + 1,
            jnp.where(first, 2, jnp.where(last, 2 * u, 2 * u + 2)))


def _attn_kernel(q_ref, k0_ref, k1_ref, k2_ref, k3_ref, v0_ref, v1_ref, v2_ref, v3_ref,
                 bias_ref, o_ref, s_scr, p_scr, psum_scr, o_scr):
    u = pl.program_id(1)
    k_refs = (k0_ref, k1_ref, k2_ref, k3_ref)
    v_refs = (v0_ref, v1_ref, v2_ref, v3_ref)
    w = V7X_MXU_DIM
    half = V7X_LANES
    slot_off = [KEY_BLOCK_ROWS * blk - Q_ROWS * u + WIN_H - 1
                for blk in _key_slot_blocks(u, pl.num_programs(1))]
    lane = lax.broadcasted_iota(jnp.int32, (GRID_W, half), 1)
    low_head = lane < HEAD_DIM
    word_lane = lax.broadcasted_iota(jnp.int32, (GRID_W // 2, half), 1)
    all_bits = jnp.uint32(0xFFFFFFFF)
    keep_low = jnp.where(word_lane < HEAD_DIM, all_bits, jnp.uint32(0))
    keep_high = jnp.where(word_lane < HEAD_DIM, jnp.uint32(0), all_bits)
    zero_half = jnp.zeros((GRID_W // 2, half), jnp.uint32)
    r_id = lax.broadcasted_iota(jnp.int32, (w, w), 0) // HEAD_DIM
    c_id = lax.broadcasted_iota(jnp.int32, (w, w), 1) // HEAD_DIM
    ones_bd = jnp.where(r_id == c_id, 1.0, 0.0).astype(BF16)
    users = [(j, max(0, j - WIN_H + 1), min(Q_ROWS, j + 1)) for j in range(WIN_TILES)]

    def tile(refs, j, lanes):
        slot, row = divmod(j, KEY_BLOCK_ROWS)
        t = refs[slot][0, row * GRID_W:(row + 1) * GRID_W, lanes]
        t = pltpu.bitcast(t, jnp.uint32)
        t01, t23 = t[:, :half], t[:, half:]
        bd = jnp.concatenate([
            jnp.concatenate([t01 & keep_low, zero_half], axis=1),
            jnp.concatenate([t01 & keep_high, zero_half], axis=1),
            jnp.concatenate([zero_half, t23 & keep_low], axis=1),
            jnp.concatenate([zero_half, t23 & keep_high], axis=1)], axis=0)
        return pltpu.bitcast(bd, BF16)

    def head_max(m1):
        cols = []
        for c in range(w // half):
            x = m1[:, c * half:(c + 1) * half]
            lo = jnp.max(jnp.where(low_head, x, NEG_INF), axis=-1, keepdims=True)
            hi = jnp.max(jnp.where(low_head, NEG_INF, x), axis=-1, keepdims=True)
            cols.append(jnp.where(low_head, lo, hi))
        return jnp.concatenate(cols, axis=-1)

    def scores(g):
        sl = g % N_ATTN_SLOTS
        lanes = slice(g * w, (g + 1) * w)
        for j, ilo, ihi in users:
            kbd = tile(k_refs, j, lanes)
            s = lax.dot_general(q_ref[0, ilo * GRID_W:ihi * GRID_W, lanes], kbd,
                                (((1,), (1,)), ((), ())), preferred_element_type=F32)
            for i in range(ilo, ihi):
                d = slot_off[j // KEY_BLOCK_ROWS] + (j % KEY_BLOCK_ROWS - i)
                s_scr[sl, i, j - i] = (s[(i - ilo) * GRID_W:(i - ilo + 1) * GRID_W]
                                       + bias_ref[g, d])

    def softmax_numerators(g):
        sl = g % N_ATTN_SLOTS
        for i in range(Q_ROWS):
            m1 = s_scr[sl, i, 0]
            for jj in range(1, WIN_H):
                m1 = jnp.maximum(m1, s_scr[sl, i, jj])
            m = head_max(m1)
            psum = jnp.zeros((GRID_W, w), F32)
            for jj in range(WIN_H):
                p = jnp.exp2(s_scr[sl, i, jj] - m)
                psum = psum + p
                p_scr[sl, i + jj, i * GRID_W:(i + 1) * GRID_W, :] = p.astype(BF16)
            psum_scr[sl, i * GRID_W:(i + 1) * GRID_W, :] = psum

    def weighted_values(g):
        sl = g % N_ATTN_SLOTS
        lanes = slice(g * w, (g + 1) * w)
        for j, ilo, ihi in users:
            vbd = tile(v_refs, j, lanes)
            pv = _dot(p_scr[sl, j, ilo * GRID_W:ihi * GRID_W, :], vbd)
            for i in range(ilo, ihi):
                rows = slice(i * GRID_W, (i + 1) * GRID_W)
                part = pv[(i - ilo) * GRID_W:(i - ilo + 1) * GRID_W]
                if j == i:
                    o_scr[sl, rows, :] = part
                else:
                    o_scr[sl, rows, :] = o_scr[sl, rows, :] + part
        ps = psum_scr[sl]
        ps_hi = ps.astype(BF16)
        ps_lo = (ps - ps_hi.astype(F32)).astype(BF16)
        denom = _dot(ps_hi, ones_bd) + _dot(ps_lo, ones_bd)
        o_ref[0, :, lanes] = (o_scr[sl] / denom).astype(BF16)

    scores(0)
    for g in range(N_HEAD_GROUPS):
        if g + 1 < N_HEAD_GROUPS:
            scores(g + 1)
        softmax_numerators(g)
        weighted_values(g)


def _attention(q, k, v, bias):
    bsz, t, d = q.shape
    assert t % Q_TOK == 0 and t // Q_TOK >= 2
    n_steps = t // Q_TOK
    w = V7X_MXU_DIM
    tok = pl.BlockSpec((1, Q_TOK, d), lambda b, u: (b, u, 0))

    def key_slot(slot):
        return pl.BlockSpec((1, KEY_BLOCK_TOK, d),
                            lambda b, u: (b, _key_slot_blocks(u, n_steps)[slot], 0))

    key_slots = [key_slot(s) for s in range(N_KEY_SLOTS)]
    return pl.pallas_call(
        _attn_kernel,
        out_shape=jax.ShapeDtypeStruct((bsz, t, d), BF16),
        grid=(bsz, n_steps),
        in_specs=[tok] + key_slots + key_slots
                 + [_resident(bias.shape)],
        out_specs=tok,
        scratch_shapes=[
            pltpu.VMEM((N_ATTN_SLOTS, Q_ROWS, WIN_H, GRID_W, w), F32),
            pltpu.VMEM((N_ATTN_SLOTS, WIN_TILES, Q_TOK, w), BF16),
            pltpu.VMEM((N_ATTN_SLOTS, Q_TOK, w), F32),
            pltpu.VMEM((N_ATTN_SLOTS, Q_TOK, w), F32),
        ],
        compiler_params=_params(2),
        name="neighbourhood_attention",
    )(q, *([k] * N_KEY_SLOTS), *([v] * N_KEY_SLOTS), bias)


def _halo_specs(t, d):
    tm = TOKEN_TILE
    per_tile = tm // HALO
    n_halo_blocks = t // HALO
    main = pl.BlockSpec((1, tm, d), lambda b, i: (b, i, 0))
    prev = pl.BlockSpec((1, HALO, d), lambda b, i: (b, jnp.maximum(i * per_tile - 1, 0), 0))
    nxt = pl.BlockSpec(
        (1, HALO, d), lambda b, i: (b, jnp.minimum((i + 1) * per_tile, n_halo_blocks - 1), 0))
    return main, prev, nxt


def _sub_tile_rows(prev_ref, main_ref, next_ref, s):
    lo = s * SUB_TILE
    hi = lo + SUB_TILE
    before = prev_ref[0] if s == 0 else main_ref[0, lo - HALO:lo, :]
    after = next_ref[0] if s == N_SUB - 1 else main_ref[0, hi:hi + HALO, :]
    return jnp.concatenate([before, main_ref[0, lo:hi, :], after], axis=0)


def _sub_tile_inside(s):
    lo_inside = True if s > 0 else pl.program_id(1) > 0
    hi_inside = True if s < N_SUB - 1 else pl.program_id(1) < pl.num_programs(1) - 1
    return lo_inside, hi_inside


def _run_interleaved(chains):
    live = list(chains)
    while live:
        for chain in list(live):
            try:
                next(chain)
            except StopIteration:
                live.remove(chain)


def _conv3(a, w, lo, n):
    rows = a.shape[0]
    prev = pltpu.roll(a, 1, 0)
    nxt = pltpu.roll(a, rows - 1, 0)
    return prev[lo:lo + n] * w[0:1] + a[lo:lo + n] * w[1:2] + nxt[lo:lo + n] * w[2:3]


def _conv_ffn_stages(x1, halo, inside, mod, g, wup_ref, cw_ref, cb_ref, wdn_ref, out):
    tm = SUB_TILE
    h2 = _modnorm(x1, g, mod[4:5], mod[3:4])
    h2 = _zero_outside_sequence(h2, halo, *inside)
    h2_ext = h2.astype(BF16)
    h2_main = h2[halo:halo + tm].astype(BF16)
    yield

    def up(c0, c1):
        return (_dot(h2_ext, wup_ref[0, :, c0:c1]),
                _dot(h2_main, wup_ref[0, :, D_FF + c0:D_FF + c1]))

    acc = None
    ahead = up(*FFN_CHUNKS[0])
    for c, (c0, c1) in enumerate(FFN_CHUNKS):
        a, val = ahead
        if c + 1 < len(FFN_CHUNKS):
            ahead = up(*FFN_CHUNKS[c + 1])
        yield
        a = _conv3(a, cw_ref[0, :, c0:c1], halo, tm) + cb_ref[0, :, c0:c1]
        gated = (_silu(a) * val).astype(BF16)
        yield
        part = _dot(gated, wdn_ref[0, c0:c1, :])
        acc = part if acc is None else acc + part
    out.append(acc)


def _attn_out_ffn_kernel(xm_ref, xp_ref, xn_ref, om_ref, op_ref, on_ref, mod_ref, g_ref,
                         wo_ref, wup_ref, cw_ref, cb_ref, wdn_ref, out_ref):
    mod = mod_ref[0, 0]

    def chain(s):
        x = _sub_tile_rows(xp_ref, xm_ref, xn_ref, s)
        o = _sub_tile_rows(op_ref, om_ref, on_ref, s)
        x1 = x + mod[2:3] * _dot(o, wo_ref[0])
        yield
        ffn = []
        yield from _conv_ffn_stages(x1, HALO, _sub_tile_inside(s), mod, g_ref[0],
                                    wup_ref, cw_ref, cb_ref, wdn_ref, ffn)
        out_ref[0, s * SUB_TILE:(s + 1) * SUB_TILE, :] = (
            x1[HALO:HALO + SUB_TILE] + mod[5:6] * ffn[0])

    _run_interleaved([chain(s) for s in range(N_SUB)])


def _attn_out_ffn(x, o, mod, seq0, layer, na_layer, g2, w_o, w_up, conv_w, conv_b, w_down):
    bsz, t, d = x.shape
    main, prev, nxt = _halo_specs(t, d)
    return pl.pallas_call(
        _attn_out_ffn_kernel,
        out_shape=jax.ShapeDtypeStruct((bsz, t, d), F32),
        grid=(bsz, t // TOKEN_TILE),
        in_specs=[
            main, prev, nxt, main, prev, nxt,
            pl.BlockSpec((1, 1, 6, d), lambda b, i: (layer, seq0 + b, 0, 0)),
            _layer_block(g2, layer),
            _layer_block(w_o, na_layer),
            _layer_block(w_up, layer),
            _layer_block(conv_w, layer),
            _layer_block(conv_b, layer),
            _layer_block(w_down, layer),
        ],
        out_specs=main,
        compiler_params=_params(2),
        name="attn_out_conv_ffn",
    )(x, x, x, o, o, o, mod, g2, w_o, w_up, conv_w, conv_b, w_down)


def _shortconv_layer_kernel(xm_ref, xp_ref, xn_ref, mod_ref, g1_ref, g2_ref, gf_ref,
                            win_ref, scw_ref, wout_ref, wup_ref, cw_ref, cb_ref, wdn_ref,
                            out_ref):
    tm = SUB_TILE
    d = D_MODEL
    inner = HALO // 2
    n1 = tm + 2 * inner
    lo = HALO - inner
    mod = mod_ref[0, 0]
    scw = scw_ref[0]

    def chain(s):
        inside = _sub_tile_inside(s)
        x = _sub_tile_rows(xp_ref, xm_ref, xn_ref, s)
        h1 = _modnorm(x, g1_ref[0], mod[1:2], mod[0:1]).astype(BF16)
        yield
        cg = _dot(h1, win_ref[0, :, d:2 * d])
        uu = _dot(h1, win_ref[0, :, 2 * d:])
        bg = _dot(h1, win_ref[0, :, 0:d])
        yield
        cu = _zero_outside_sequence(cg * uu, HALO, *inside)
        z = (bg[lo:lo + n1] * _conv3(cu, scw, lo, n1)).astype(BF16)
        yield
        x1 = x[lo:lo + n1] + mod[2:3] * _dot(z, wout_ref[0])
        yield
        ffn = []
        yield from _conv_ffn_stages(x1, inner, inside, mod, g2_ref[0],
                                    wup_ref, cw_ref, cb_ref, wdn_ref, ffn)
        x2 = x1[inner:inner + tm] + mod[5:6] * ffn[0]
        out_ref[0, s * tm:(s + 1) * tm, :] = _rmsnorm(x2, gf_ref[...])

    _run_interleaved([chain(s) for s in range(N_SUB)])


def _shortconv_layer(x, mod, seq0, layer, sc_layer, g1, g2, gf, w_in, sc_conv_w, w_out,
                     w_up, conv_w, conv_b, w_down):
    bsz, t, d = x.shape
    main, prev, nxt = _halo_specs(t, d)
    return pl.pallas_call(
        _shortconv_layer_kernel,
        out_shape=jax.ShapeDtypeStruct((bsz, t, d), F32),
        grid=(bsz, t // TOKEN_TILE),
        in_specs=[
            main, prev, nxt,
            pl.BlockSpec((1, 1, 6, d), lambda b, i: (layer, seq0 + b, 0, 0)),
            _layer_block(g1, layer), _layer_block(g2, layer), _resident((1, d)),
            _layer_block(w_in, sc_layer),
            _layer_block(sc_conv_w, sc_layer),
            _layer_block(w_out, sc_layer),
            _layer_block(w_up, layer),
            _layer_block(conv_w, layer),
            _layer_block(conv_b, layer),
            _layer_block(w_down, layer),
        ],
        out_specs=main,
        compiler_params=_params(2),
        name="shortconv_layer",
    )(x, x, x, mod, g1, g2, gf, w_in, sc_conv_w, w_out, w_up, conv_w, conv_b, w_down)


def _trunk(x, seq0, mod, p):
    assert x.shape[1] % TOKEN_TILE == 0
    q, k, v = _qkv_proj(x, mod, seq0, 0, 0, p["norm1_g"], p["w_qkv"], p["b_qkv"])
    o = _attention(q, k, v, p["attn_bias"])
    x = _attn_out_ffn(x, o, mod, seq0, 0, 0, p["norm2_g"], p["w_o"], p["w_up"],
                      p["ffn_conv_w"], p["ffn_conv_b"], p["w_down"])
    return _shortconv_layer(x, mod, seq0, 1, 0, p["norm1_g"], p["norm2_g"], p["final_g"],
                            p["w_in"], p["sc_conv_w"], p["w_out"], p["w_up"],
                            p["ffn_conv_w"], p["ffn_conv_b"], p["w_down"])


def kernel(x_prompt, x_sample, c_prompt, c_sample, ada_w, ada_b, norm1_g, norm2_g,
           na_w_qkv, na_b_qkv, na_rpb, na_w_o, sc_w_in, sc_conv_w, sc_w_out,
           ffn_w_up, ffn_conv_w, ffn_conv_b, ffn_w_down, final_g):
    depth = ada_w.shape[0]
    n_prompt = c_prompt.shape[0]
    n_seq = n_prompt + c_sample.shape[0]
    n_pad = -n_seq % 8
    c_all = jnp.concatenate([c_prompt, c_sample, jnp.zeros((n_pad, D_MODEL), F32)], axis=0)
    mod = _modulation(c_all, ada_w, ada_b).reshape(depth, n_seq + n_pad, 6, D_MODEL)
    p = dict(
        norm1_g=norm1_g.reshape(depth, 1, D_MODEL), norm2_g=norm2_g.reshape(depth, 1, D_MODEL),
        final_g=final_g.reshape(1, D_MODEL),
        w_qkv=na_w_qkv.astype(BF16), b_qkv=na_b_qkv.reshape(-1, 1, 3 * D_MODEL),
        attn_bias=_attn_bias_table(na_rpb[0]),
        w_o=na_w_o.astype(BF16),
        w_in=sc_w_in.astype(BF16), sc_conv_w=sc_conv_w, w_out=sc_w_out.astype(BF16),
        w_up=ffn_w_up.astype(BF16), ffn_conv_w=ffn_conv_w,
        ffn_conv_b=ffn_conv_b.reshape(depth, 1, D_FF),
        w_down=ffn_w_down.astype(BF16),
    )
    y_prompt = _trunk(x_prompt, 0, mod, p)
    y_sample = _trunk(x_sample, n_prompt, mod, p)
    return (y_prompt, y_sample)
```

```python
import math

import numpy as np
import jax
import jax.numpy as jnp
from jax import lax
from jax.experimental import pallas as pl
from jax.experimental.pallas import tpu as pltpu

F32 = jnp.float32
BF16 = jnp.bfloat16

D_MODEL = 1024
N_HEADS = 16
HEAD_DIM = D_MODEL // N_HEADS
GRID_W = 64
WIN_H = 8
WIN_W = 16
RPB_ROWS = 2 * WIN_H - 1
RPB_COLS = 2 * WIN_W - 1
D_FF = 2816
EPS = 1e-6
NEG_INF = -1e30
LOG2E = math.log2(math.e)

V7X_LANES = 128
V7X_MXU_DIM = 256
V7X_VMEM_LIMIT_BYTES = 56 * 1024 * 1024

HEADS_PER_GROUP = V7X_MXU_DIM // HEAD_DIM
N_HEAD_GROUPS = N_HEADS // HEADS_PER_GROUP
Q_ROWS = 8
Q_TOK = Q_ROWS * GRID_W
KEY_BLOCK_ROWS = 4
KEY_BLOCK_TOK = KEY_BLOCK_ROWS * GRID_W
N_KEY_SLOTS = (Q_ROWS + WIN_H) // KEY_BLOCK_ROWS
WIN_TILES = Q_ROWS + WIN_H - 1
N_ATTN_SLOTS = 2

TOKEN_TILE = 512
SUB_TILE = 512
N_SUB = TOKEN_TILE // SUB_TILE
HALO = 16
FFN_CHUNKS = ((0, 768), (768, 1536), (1536, 2304), (2304, D_FF))


def _dot(a, b):
    return jnp.dot(a, b, preferred_element_type=F32)


def _silu(a):
    return a / (1.0 + jnp.exp(-a))


def _rmsnorm(x, g):
    ms = jnp.mean(x * x, axis=-1, keepdims=True)
    return x * lax.rsqrt(ms + EPS) * g


def _modnorm(x, g, scale, shift):
    return _rmsnorm(x, g) * (1.0 + scale) + shift


def _zero_outside_sequence(v, halo, lo_inside, hi_inside):
    if lo_inside is True and hi_inside is True:
        return v
    n_rows = v.shape[0]
    row = lax.broadcasted_iota(jnp.int32, (n_rows, 1), 0)
    lo_ok = 0 if lo_inside is True else jnp.where(lo_inside, 0, halo)
    hi_ok = n_rows if hi_inside is True else jnp.where(hi_inside, n_rows, n_rows - halo)
    return jnp.where((row >= lo_ok) & (row < hi_ok), v, 0.0)


def _resident(shape, index=None):
    index = (0,) * len(shape) if index is None else index
    return pl.BlockSpec(shape, lambda *_: index, pipeline_mode=pl.Buffered(1))


def _layer_block(arr, layer):
    return _resident((1,) + arr.shape[1:], (layer,) + (0,) * (arr.ndim - 1))


def _params(dims):
    return pltpu.CompilerParams(dimension_semantics=("parallel",) * dims,
                                vmem_limit_bytes=V7X_VMEM_LIMIT_BYTES)


def _mod_kernel(c_ref, w_ref, b_ref, o_ref):
    s = _silu(c_ref[...])
    s_hi = s.astype(BF16)
    s_lo = (s - s_hi.astype(F32)).astype(BF16)
    w = w_ref[0]
    w_hi = w.astype(BF16)
    w_lo = (w - w_hi.astype(F32)).astype(BF16)
    acc = _dot(s_hi, w_hi) + _dot(s_lo, w_hi) + _dot(s_hi, w_lo)
    o_ref[0] = acc + b_ref[0]


def _modulation(c_all, ada_w, ada_b):
    depth = ada_w.shape[0]
    n_seq = c_all.shape[0]
    cn = D_MODEL
    return pl.pallas_call(
        _mod_kernel,
        out_shape=jax.ShapeDtypeStruct((depth, n_seq, 6 * D_MODEL), F32),
        grid=(depth, 6 * D_MODEL // cn),
        in_specs=[
            pl.BlockSpec((n_seq, D_MODEL), lambda l, n: (0, 0)),
            pl.BlockSpec((1, D_MODEL, cn), lambda l, n: (l, 0, n)),
            pl.BlockSpec((1, 1, cn), lambda l, n: (l, 0, n)),
        ],
        out_specs=pl.BlockSpec((1, n_seq, cn), lambda l, n: (l, 0, n)),
        compiler_params=_params(2),
        name="adaln_modulation",
    )(c_all, ada_w, ada_b.reshape(depth, 1, 6 * D_MODEL))


def _qkv_kernel(x_ref, mod_ref, g_ref, w_ref, b_ref, q_ref, k_ref, v_ref):
    mod = mod_ref[0, 0]
    h = _modnorm(x_ref[0], g_ref[0], mod[1:2], mod[0:1]).astype(BF16)
    d = D_MODEL
    q = (_dot(h, w_ref[0, :, 0:d]) + b_ref[0, :, 0:d]) * (HEAD_DIM ** -0.5 * LOG2E)
    q_ref[0] = q.astype(BF16)
    k_ref[0] = (_dot(h, w_ref[0, :, d:2 * d]) + b_ref[0, :, d:2 * d]).astype(BF16)
    v_ref[0] = (_dot(h, w_ref[0, :, 2 * d:]) + b_ref[0, :, 2 * d:]).astype(BF16)


def _qkv_proj(x, mod, seq0, layer, na_layer, g, w_qkv, b_qkv):
    bsz, t, d = x.shape
    tm = TOKEN_TILE
    tok = pl.BlockSpec((1, tm, d), lambda b, i: (b, i, 0))
    out = jax.ShapeDtypeStruct((bsz, t, d), BF16)
    return pl.pallas_call(
        _qkv_kernel,
        out_shape=(out, out, out),
        grid=(bsz, t // tm),
        in_specs=[
            tok,
            pl.BlockSpec((1, 1, 6, d), lambda b, i: (layer, seq0 + b, 0, 0)),
            _layer_block(g, layer),
            _layer_block(w_qkv, na_layer),
            _layer_block(b_qkv, na_layer),
        ],
        out_specs=(tok, tok, tok),
        compiler_params=_params(2),
        name="norm_qkv",
    )(x, mod, g, w_qkv, b_qkv)


def _bias_kernel(rpb_ref, o_ref):
    shape = (GRID_W, V7X_LANES)
    qc = lax.broadcasted_iota(jnp.int32, shape, 0)
    lane = lax.broadcasted_iota(jnp.int32, shape, 1)
    kc = lane & (GRID_W - 1)
    start = jnp.clip(qc - WIN_W // 2, 0, GRID_W - WIN_W)
    valid = (kc >= start) & (kc < start + WIN_W)
    low_head = lane < HEAD_DIM
    shift_low = V7X_LANES - (WIN_W - 1)
    shift_high = (shift_low + HEAD_DIM) % V7X_LANES
    for d in range(RPB_ROWS):
        halves = []
        for c in range(HEADS_PER_GROUP // 2):
            r_low = jnp.broadcast_to(rpb_ref[0, 2 * c, d:d + 1, :], shape)
            r_high = jnp.broadcast_to(rpb_ref[0, 2 * c + 1, d:d + 1, :], shape)
            t_low = pltpu.roll(r_low, shift_low, 1, stride=1, stride_axis=0)
            t_high = pltpu.roll(r_high, shift_high, 1, stride=1, stride_axis=0)
            b = jnp.where(low_head, t_low, t_high) * LOG2E
            halves.append(jnp.where(valid, b, NEG_INF))
        o_ref[0, d] = jnp.concatenate(halves, axis=1)


def _attn_bias_table(rpb):
    r = jnp.pad(rpb.astype(F32), ((0, 0), (0, 0), (0, V7X_LANES - RPB_COLS)))
    r = r.reshape(N_HEAD_GROUPS, HEADS_PER_GROUP, RPB_ROWS, V7X_LANES)
    w = HEADS_PER_GROUP * GRID_W
    return pl.pallas_call(
        _bias_kernel,
        out_shape=jax.ShapeDtypeStruct((N_HEAD_GROUPS, RPB_ROWS, GRID_W, w), F32),
        grid=(N_HEAD_GROUPS,),
        in_specs=[pl.BlockSpec((1, HEADS_PER_GROUP, RPB_ROWS, V7X_LANES),
                               lambda g: (g, 0, 0, 0))],
        out_specs=pl.BlockSpec((1, RPB_ROWS, GRID_W, w), lambda g: (g, 0, 0, 0)),
        compiler_params=_params(1),
        name="attn_bias_table",
    )(r)


def _key_slot_blocks(u, n_steps):
    first = u == 0
    last = u == n_steps - 1
    return (jnp.where(first, 1, 2 * u - 1),
            2 * u,
            2 * u + 1,
            jnp.where(first, 2, jnp.where(last, 2 * u, 2 * u + 2)))


def _attn_kernel(q_ref, k0_ref, k1_ref, k2_ref, k3_ref, v0_ref, v1_ref, v2_ref, v3_ref,
                 bias_ref, o_ref, s_scr, p_scr, psum_scr, o_scr):
    u = pl.program_id(1)
    k_refs = (k0_ref, k1_ref, k2_ref, k3_ref)
    v_refs = (v0_ref, v1_ref, v2_ref, v3_ref)
    w = V7X_MXU_DIM
    half = V7X_LANES
    slot_off = [KEY_BLOCK_ROWS * blk - Q_ROWS * u + WIN_H - 1
                for blk in _key_slot_blocks(u, pl.num_programs(1))]
    lane = lax.broadcasted_iota(jnp.int32, (GRID_W, half), 1)
    low_head = lane < HEAD_DIM
    keep_low = jnp.where(low_head, 1.0, 0.0).astype(BF16)
    keep_high = jnp.where(low_head, 0.0, 1.0).astype(BF16)
    zero_half = jnp.zeros((GRID_W, half), BF16)
    r_id = lax.broadcasted_iota(jnp.int32, (w, w), 0) // HEAD_DIM
    c_id = lax.broadcasted_iota(jnp.int32, (w, w), 1) // HEAD_DIM
    ones_bd = jnp.where(r_id == c_id, 1.0, 0.0).astype(BF16)
    users = [(j, max(0, j - WIN_H + 1), min(Q_ROWS, j + 1)) for j in range(WIN_TILES)]

    def tile(refs, j, lanes):
        slot, row = divmod(j, KEY_BLOCK_ROWS)
        t = refs[slot][0, row * GRID_W:(row + 1) * GRID_W, lanes]
        t01, t23 = t[:, :half], t[:, half:]
        return jnp.concatenate([
            jnp.concatenate([t01 * keep_low, zero_half], axis=1),
            jnp.concatenate([t01 * keep_high, zero_half], axis=1),
            jnp.concatenate([zero_half, t23 * keep_low], axis=1),
            jnp.concatenate([zero_half, t23 * keep_high], axis=1)], axis=0)

    def head_max(m1):
        cols = []
        for c in range(w // half):
            x = m1[:, c * half:(c + 1) * half]
            lo = jnp.max(jnp.where(low_head, x, NEG_INF), axis=-1, keepdims=True)
            hi = jnp.max(jnp.where(low_head, NEG_INF, x), axis=-1, keepdims=True)
            cols.append(jnp.where(low_head, lo, hi))
        return jnp.concatenate(cols, axis=-1)

    def scores(g):
        sl = g % N_ATTN_SLOTS
        lanes = slice(g * w, (g + 1) * w)
        for j, ilo, ihi in users:
            kbd = tile(k_refs, j, lanes)
            s = lax.dot_general(q_ref[0, ilo * GRID_W:ihi * GRID_W, lanes], kbd,
                                (((1,), (1,)), ((), ())), preferred_element_type=F32)
            for i in range(ilo, ihi):
                d = slot_off[j // KEY_BLOCK_ROWS] + (j % KEY_BLOCK_ROWS - i)
                s_scr[sl, i, j - i] = (s[(i - ilo) * GRID_W:(i - ilo + 1) * GRID_W]
                                       + bias_ref[g, d])

    def softmax_numerators(g):
        sl = g % N_ATTN_SLOTS
        for i in range(Q_ROWS):
            m1 = s_scr[sl, i, 0]
            for jj in range(1, WIN_H):
                m1 = jnp.maximum(m1, s_scr[sl, i, jj])
            m = head_max(m1)
            psum = jnp.zeros((GRID_W, w), F32)
            for jj in range(WIN_H):
                p = jnp.exp2(s_scr[sl, i, jj] - m)
                psum = psum + p
                p_scr[sl, i + jj, i * GRID_W:(i + 1) * GRID_W, :] = p.astype(BF16)
            psum_scr[sl, i * GRID_W:(i + 1) * GRID_W, :] = psum

    def weighted_values(g):
        sl = g % N_ATTN_SLOTS
        lanes = slice(g * w, (g + 1) * w)
        for j, ilo, ihi in users:
            vbd = tile(v_refs, j, lanes)
            pv = _dot(p_scr[sl, j, ilo * GRID_W:ihi * GRID_W, :], vbd)
            for i in range(ilo, ihi):
                rows = slice(i * GRID_W, (i + 1) * GRID_W)
                part = pv[(i - ilo) * GRID_W:(i - ilo + 1) * GRID_W]
                if j == i:
                    o_scr[sl, rows, :] = part
                else:
                    o_scr[sl, rows, :] = o_scr[sl, rows, :] + part
        ps = psum_scr[sl]
        ps_hi = ps.astype(BF16)
        ps_lo = (ps - ps_hi.astype(F32)).astype(BF16)
        denom = _dot(ps_hi, ones_bd) + _dot(ps_lo, ones_bd)
        o_ref[0, :, lanes] = (o_scr[sl] / denom).astype(BF16)

    for g in range(min(N_ATTN_SLOTS, N_HEAD_GROUPS)):
        scores(g)
    for g in range(N_HEAD_GROUPS):
        softmax_numerators(g)
        if g + N_ATTN_SLOTS < N_HEAD_GROUPS:
            scores(g + N_ATTN_SLOTS)
        weighted_values(g)


def _attention(q, k, v, bias):
    bsz, t, d = q.shape
    assert t % Q_TOK == 0 and t // Q_TOK >= 2
    n_steps = t // Q_TOK
    w = V7X_MXU_DIM
    tok = pl.BlockSpec((1, Q_TOK, d), lambda b, u: (b, u, 0))

    def key_slot(slot):
        return pl.BlockSpec((1, KEY_BLOCK_TOK, d),
                            lambda b, u: (b, _key_slot_blocks(u, n_steps)[slot], 0))

    key_slots = [key_slot(s) for s in range(N_KEY_SLOTS)]
    return pl.pallas_call(
        _attn_kernel,
        out_shape=jax.ShapeDtypeStruct((bsz, t, d), BF16),
        grid=(bsz, n_steps),
        in_specs=[tok] + key_slots + key_slots
                 + [_resident(bias.shape)],
        out_specs=tok,
        scratch_shapes=[
            pltpu.VMEM((N_ATTN_SLOTS, Q_ROWS, WIN_H, GRID_W, w), F32),
            pltpu.VMEM((N_ATTN_SLOTS, WIN_TILES, Q_TOK, w), BF16),
            pltpu.VMEM((N_ATTN_SLOTS, Q_TOK, w), F32),
            pltpu.VMEM((N_ATTN_SLOTS, Q_TOK, w), F32),
        ],
        compiler_params=_params(2),
        name="neighbourhood_attention",
    )(q, *([k] * N_KEY_SLOTS), *([v] * N_KEY_SLOTS), bias)


def _halo_specs(t, d):
    tm = TOKEN_TILE
    per_tile = tm // HALO
    n_halo_blocks = t // HALO
    main = pl.BlockSpec((1, tm, d), lambda b, i: (b, i, 0))
    prev = pl.BlockSpec((1, HALO, d), lambda b, i: (b, jnp.maximum(i * per_tile - 1, 0), 0))
    nxt = pl.BlockSpec(
        (1, HALO, d), lambda b, i: (b, jnp.minimum((i + 1) * per_tile, n_halo_blocks - 1), 0))
    return main, prev, nxt


def _sub_tile_rows(prev_ref, main_ref, next_ref, s):
    lo = s * SUB_TILE
    hi = lo + SUB_TILE
    before = prev_ref[0] if s == 0 else main_ref[0, lo - HALO:lo, :]
    after = next_ref[0] if s == N_SUB - 1 else main_ref[0, hi:hi + HALO, :]
    return jnp.concatenate([before, main_ref[0, lo:hi, :], after], axis=0)


def _sub_tile_inside(s):
    lo_inside = True if s > 0 else pl.program_id(1) > 0
    hi_inside = True if s < N_SUB - 1 else pl.program_id(1) < pl.num_programs(1) - 1
    return lo_inside, hi_inside


def _run_interleaved(chains):
    live = list(chains)
    while live:
        for chain in list(live):
            try:
                next(chain)
            except StopIteration:
                live.remove(chain)


def _conv3(a, w, lo, n):
    rows = a.shape[0]
    prev = pltpu.roll(a, 1, 0)
    nxt = pltpu.roll(a, rows - 1, 0)
    return prev[lo:lo + n] * w[0:1] + a[lo:lo + n] * w[1:2] + nxt[lo:lo + n] * w[2:3]


def _conv_ffn_stages(x1, halo, inside, mod, g, wup_ref, cw_ref, cb_ref, wdn_ref, out):
    tm = SUB_TILE
    h2 = _modnorm(x1, g, mod[4:5], mod[3:4])
    h2 = _zero_outside_sequence(h2, halo, *inside)
    h2_ext = h2.astype(BF16)
    h2_main = h2[halo:halo + tm].astype(BF16)
    yield

    def up(c0, c1):
        return (_dot(h2_ext, wup_ref[0, :, c0:c1]),
                _dot(h2_main, wup_ref[0, :, D_FF + c0:D_FF + c1]))

    acc = None
    ahead = up(*FFN_CHUNKS[0])
    for c, (c0, c1) in enumerate(FFN_CHUNKS):
        a, val = ahead
        if c + 1 < len(FFN_CHUNKS):
            ahead = up(*FFN_CHUNKS[c + 1])
        yield
        a = _conv3(a, cw_ref[0, :, c0:c1], halo, tm) + cb_ref[0, :, c0:c1]
        gated = (_silu(a) * val).astype(BF16)
        yield
        part = _dot(gated, wdn_ref[0, c0:c1, :])
        acc = part if acc is None else acc + part
    out.append(acc)


def _attn_out_ffn_kernel(xm_ref, xp_ref, xn_ref, om_ref, op_ref, on_ref, mod_ref, g_ref,
                         wo_ref, wup_ref, cw_ref, cb_ref, wdn_ref, out_ref):
    mod = mod_ref[0, 0]

    def chain(s):
        x = _sub_tile_rows(xp_ref, xm_ref, xn_ref, s)
        o = _sub_tile_rows(op_ref, om_ref, on_ref, s)
        x1 = x + mod[2:3] * _dot(o, wo_ref[0])
        yield
        ffn = []
        yield from _conv_ffn_stages(x1, HALO, _sub_tile_inside(s), mod, g_ref[0],
                                    wup_ref, cw_ref, cb_ref, wdn_ref, ffn)
        out_ref[0, s * SUB_TILE:(s + 1) * SUB_TILE, :] = (
            x1[HALO:HALO + SUB_TILE] + mod[5:6] * ffn[0])

    _run_interleaved([chain(s) for s in range(N_SUB)])


def _attn_out_ffn(x, o, mod, seq0, layer, na_layer, g2, w_o, w_up, conv_w, conv_b, w_down):
    bsz, t, d = x.shape
    main, prev, nxt = _halo_specs(t, d)
    return pl.pallas_call(
        _attn_out_ffn_kernel,
        out_shape=jax.ShapeDtypeStruct((bsz, t, d), F32),
        grid=(bsz, t // TOKEN_TILE),
        in_specs=[
            main, prev, nxt, main, prev, nxt,
            pl.BlockSpec((1, 1, 6, d), lambda b, i: (layer, seq0 + b, 0, 0)),
            _layer_block(g2, layer),
            _layer_block(w_o, na_layer),
            _layer_block(w_up, layer),
            _layer_block(conv_w, layer),
            _layer_block(conv_b, layer),
            _layer_block(w_down, layer),
        ],
        out_specs=main,
        compiler_params=_params(2),
        name="attn_out_conv_ffn",
    )(x, x, x, o, o, o, mod, g2, w_o, w_up, conv_w, conv_b, w_down)


def _shortconv_layer_kernel(xm_ref, xp_ref, xn_ref, mod_ref, g1_ref, g2_ref, gf_ref,
                            win_ref, scw_ref, wout_ref, wup_ref, cw_ref, cb_ref, wdn_ref,
                            out_ref):
    tm = SUB_TILE
    d = D_MODEL
    inner = HALO // 2
    n1 = tm + 2 * inner
    lo = HALO - inner
    mod = mod_ref[0, 0]
    scw = scw_ref[0]

    def chain(s):
        inside = _sub_tile_inside(s)
        x = _sub_tile_rows(xp_ref, xm_ref, xn_ref, s)
        h1 = _modnorm(x, g1_ref[0], mod[1:2], mod[0:1]).astype(BF16)
        yield
        cg = _dot(h1, win_ref[0, :, d:2 * d])
        uu = _dot(h1, win_ref[0, :, 2 * d:])
        bg = _dot(h1, win_ref[0, :, 0:d])
        yield
        cu = _zero_outside_sequence(cg * uu, HALO, *inside)
        z = (bg[lo:lo + n1] * _conv3(cu, scw, lo, n1)).astype(BF16)
        yield
        x1 = x[lo:lo + n1] + mod[2:3] * _dot(z, wout_ref[0])
        yield
        ffn = []
        yield from _conv_ffn_stages(x1, inner, inside, mod, g2_ref[0],
                                    wup_ref, cw_ref, cb_ref, wdn_ref, ffn)
        x2 = x1[inner:inner + tm] + mod[5:6] * ffn[0]
        out_ref[0, s * tm:(s + 1) * tm, :] = _rmsnorm(x2, gf_ref[...])

    _run_interleaved([chain(s) for s in range(N_SUB)])


def _shortconv_layer(x, mod, seq0, layer, sc_layer, g1, g2, gf, w_in, sc_conv_w, w_out,
                     w_up, conv_w, conv_b, w_down):
    bsz, t, d = x.shape
    main, prev, nxt = _halo_specs(t, d)
    return pl.pallas_call(
        _shortconv_layer_kernel,
        out_shape=jax.ShapeDtypeStruct((bsz, t, d), F32),
        grid=(bsz, t // TOKEN_TILE),
        in_specs=[
            main, prev, nxt,
            pl.BlockSpec((1, 1, 6, d), lambda b, i: (layer, seq0 + b, 0, 0)),
            _layer_block(g1, layer), _layer_block(g2, layer), _resident((1, d)),
            _layer_block(w_in, sc_layer),
            _layer_block(sc_conv_w, sc_layer),
            _layer_block(w_out, sc_layer),
            _layer_block(w_up, layer),
            _layer_block(conv_w, layer),
            _layer_block(conv_b, layer),
            _layer_block(w_down, layer),
        ],
        out_specs=main,
        compiler_params=_params(2),
        name="shortconv_layer",
    )(x, x, x, mod, g1, g2, gf, w_in, sc_conv_w, w_out, w_up, conv_w, conv_b, w_down)


def _trunk(x, seq0, mod, p):
    assert x.shape[1] % TOKEN_TILE == 0
    q, k, v = _qkv_proj(x, mod, seq0, 0, 0, p["norm1_g"], p["w_qkv"], p["b_qkv"])
    o = _attention(q, k, v, p["attn_bias"])
    x = _attn_out_ffn(x, o, mod, seq0, 0, 0, p["norm2_g"], p["w_o"], p["w_up"],
                      p["ffn_conv_w"], p["ffn_conv_b"], p["w_down"])
    return _shortconv_layer(x, mod, seq0, 1, 0, p["norm1_g"], p["norm2_g"], p["final_g"],
                            p["w_in"], p["sc_conv_w"], p["w_out"], p["w_up"],
                            p["ffn_conv_w"], p["ffn_conv_b"], p["w_down"])


def kernel(x_prompt, x_sample, c_prompt, c_sample, ada_w, ada_b, norm1_g, norm2_g,
           na_w_qkv, na_b_qkv, na_rpb, na_w_o, sc_w_in, sc_conv_w, sc_w_out,
           ffn_w_up, ffn_conv_w, ffn_conv_b, ffn_w_down, final_g):
    depth = ada_w.shape[0]
    n_prompt = c_prompt.shape[0]
    n_seq = n_prompt + c_sample.shape[0]
    n_pad = -n_seq % 8
    c_all = jnp.concatenate([c_prompt, c_sample, jnp.zeros((n_pad, D_MODEL), F32)], axis=0)
    mod = _modulation(c_all, ada_w, ada_b).reshape(depth, n_seq + n_pad, 6, D_MODEL)
    p = dict(
        norm1_g=norm1_g.reshape(depth, 1, D_MODEL), norm2_g=norm2_g.reshape(depth, 1, D_MODEL),
        final_g=final_g.reshape(1, D_MODEL),
        w_qkv=na_w_qkv.astype(BF16), b_qkv=na_b_qkv.reshape(-1, 1, 3 * D_MODEL),
        attn_bias=_attn_bias_table(na_rpb[0]),
        w_o=na_w_o.astype(BF16),
        w_in=sc_w_in.astype(BF16), sc_conv_w=sc_conv_w, w_out=sc_w_out.astype(BF16),
        w_up=ffn_w_up.astype(BF16), ffn_conv_w=ffn_conv_w,
        ffn_conv_b=ffn_conv_b.reshape(depth, 1, D_FF),
        w_down=ffn_w_down.astype(BF16),
    )
    y_prompt = _trunk(x_prompt, 0, mod, p)
    y_sample = _trunk(x_sample, n_prompt, mod, p)
    return (y_prompt, y_sample)
```

```python
import math

import numpy as np
import jax
import jax.numpy as jnp
from jax import lax
from jax.experimental import pallas as pl
from jax.experimental.pallas import tpu as pltpu

F32 = jnp.float32
BF16 = jnp.bfloat16

D_MODEL = 1024
N_HEADS = 16
HEAD_DIM = D_MODEL // N_HEADS
GRID_W = 64
WIN_H = 8
WIN_W = 16
RPB_ROWS = 2 * WIN_H - 1
RPB_COLS = 2 * WIN_W - 1
D_FF = 2816
EPS = 1e-6
NEG_INF = -1e30
LOG2E = math.log2(math.e)

V7X_LANES = 128
V7X_MXU_DIM = 256
V7X_VMEM_LIMIT_BYTES = 56 * 1024 * 1024

HEADS_PER_GROUP = V7X_MXU_DIM // HEAD_DIM
N_HEAD_GROUPS = N_HEADS // HEADS_PER_GROUP
Q_ROWS = 8
Q_TOK = Q_ROWS * GRID_W
KEY_BLOCK_ROWS = 4
KEY_BLOCK_TOK = KEY_BLOCK_ROWS * GRID_W
N_KEY_SLOTS = (Q_ROWS + WIN_H) // KEY_BLOCK_ROWS
WIN_TILES = Q_ROWS + WIN_H - 1
N_ATTN_SLOTS = 2

QKV_TILE = 1024
TOKEN_TILE = 512
SUB_TILE = 512
N_SUB = TOKEN_TILE // SUB_TILE
HALO = 16
FFN_CHUNKS = ((0, 768), (768, 1536), (1536, 2304), (2304, D_FF))


def _dot(a, b):
    return jnp.dot(a, b, preferred_element_type=F32)


def _silu(a):
    h = 0.5 * a
    return h + h * jnp.tanh(h)


def _rmsnorm(x, g):
    ms = jnp.mean(x * x, axis=-1, keepdims=True)
    return x * lax.rsqrt(ms + EPS) * g


def _modnorm(x, g, scale, shift):
    return _rmsnorm(x, g * (1.0 + scale)) + shift


def _zero_outside_sequence(v, halo, lo_inside, hi_inside):
    if lo_inside is True and hi_inside is True:
        return v
    n_rows = v.shape[0]
    row = lax.broadcasted_iota(jnp.int32, (n_rows, 1), 0)
    lo_ok = 0 if lo_inside is True else jnp.where(lo_inside, 0, halo)
    hi_ok = n_rows if hi_inside is True else jnp.where(hi_inside, n_rows, n_rows - halo)
    return jnp.where((row >= lo_ok) & (row < hi_ok), v, 0.0)


def _resident(shape, index=None):
    index = (0,) * len(shape) if index is None else index
    return pl.BlockSpec(shape, lambda *_: index, pipeline_mode=pl.Buffered(1))


def _layer_block(arr, layer):
    return _resident((1,) + arr.shape[1:], (layer,) + (0,) * (arr.ndim - 1))


def _params(dims):
    return pltpu.CompilerParams(dimension_semantics=("parallel",) * dims,
                                vmem_limit_bytes=V7X_VMEM_LIMIT_BYTES)


def _mod_kernel(c_ref, w_ref, b_ref, o_ref):
    s = _silu(c_ref[...])
    s_hi = s.astype(BF16)
    s_lo = (s - s_hi.astype(F32)).astype(BF16)
    w = w_ref[0]
    w_hi = w.astype(BF16)
    w_lo = (w - w_hi.astype(F32)).astype(BF16)
    acc = _dot(s_hi, w_hi) + _dot(s_lo, w_hi) + _dot(s_hi, w_lo)
    o_ref[0] = acc + b_ref[0]


def _modulation(c_all, ada_w, ada_b):
    depth = ada_w.shape[0]
    n_seq = c_all.shape[0]
    cn = D_MODEL
    return pl.pallas_call(
        _mod_kernel,
        out_shape=jax.ShapeDtypeStruct((depth, n_seq, 6 * D_MODEL), F32),
        grid=(depth, 6 * D_MODEL // cn),
        in_specs=[
            pl.BlockSpec((n_seq, D_MODEL), lambda l, n: (0, 0)),
            pl.BlockSpec((1, D_MODEL, cn), lambda l, n: (l, 0, n)),
            pl.BlockSpec((1, 1, cn), lambda l, n: (l, 0, n)),
        ],
        out_specs=pl.BlockSpec((1, n_seq, cn), lambda l, n: (l, 0, n)),
        compiler_params=_params(2),
        name="adaln_modulation",
    )(c_all, ada_w, ada_b.reshape(depth, 1, 6 * D_MODEL))


def _qkv_kernel(x_ref, mod_ref, g_ref, w_ref, b_ref, q_ref, k_ref, v_ref):
    mod = mod_ref[0, 0]
    h = _modnorm(x_ref[0], g_ref[0], mod[1:2], mod[0:1]).astype(BF16)
    d = D_MODEL
    q = (_dot(h, w_ref[0, :, 0:d]) + b_ref[0, :, 0:d]) * (HEAD_DIM ** -0.5 * LOG2E)
    q_ref[0] = q.astype(BF16)
    k_ref[0] = (_dot(h, w_ref[0, :, d:2 * d]) + b_ref[0, :, d:2 * d]).astype(BF16)
    v_ref[0] = (_dot(h, w_ref[0, :, 2 * d:]) + b_ref[0, :, 2 * d:]).astype(BF16)


def _qkv_proj(x, mod, seq0, layer, na_layer, g, w_qkv, b_qkv):
    bsz, t, d = x.shape
    tm = QKV_TILE
    tok = pl.BlockSpec((1, tm, d), lambda b, i: (b, i, 0))
    out = jax.ShapeDtypeStruct((bsz, t, d), BF16)
    return pl.pallas_call(
        _qkv_kernel,
        out_shape=(out, out, out),
        grid=(bsz, t // tm),
        in_specs=[
            tok,
            pl.BlockSpec((1, 1, 6, d), lambda b, i: (layer, seq0 + b, 0, 0)),
            _layer_block(g, layer),
            _layer_block(w_qkv, na_layer),
            _layer_block(b_qkv, na_layer),
        ],
        out_specs=(tok, tok, tok),
        compiler_params=_params(2),
        name="norm_qkv",
    )(x, mod, g, w_qkv, b_qkv)


def _bias_kernel(rpb_ref, o_ref):
    shape = (GRID_W, V7X_LANES)
    qc = lax.broadcasted_iota(jnp.int32, shape, 0)
    lane = lax.broadcasted_iota(jnp.int32, shape, 1)
    kc = lane & (GRID_W - 1)
    start = jnp.clip(qc - WIN_W // 2, 0, GRID_W - WIN_W)
    valid = (kc >= start) & (kc < start + WIN_W)
    low_head = lane < HEAD_DIM
    shift_low = V7X_LANES - (WIN_W - 1)
    shift_high = (shift_low + HEAD_DIM) % V7X_LANES
    for d in range(RPB_ROWS):
        halves = []
        for c in range(HEADS_PER_GROUP // 2):
            r_low = jnp.broadcast_to(rpb_ref[0, 2 * c, d:d + 1, :], shape)
            r_high = jnp.broadcast_to(rpb_ref[0, 2 * c + 1, d:d + 1, :], shape)
            t_low = pltpu.roll(r_low, shift_low, 1, stride=1, stride_axis=0)
            t_high = pltpu.roll(r_high, shift_high, 1, stride=1, stride_axis=0)
            b = jnp.where(low_head, t_low, t_high) * LOG2E
            halves.append(jnp.where(valid, b, NEG_INF))
        o_ref[0, d] = jnp.concatenate(halves, axis=1)


def _attn_bias_table(rpb):
    r = jnp.pad(rpb.astype(F32), ((0, 0), (0, 0), (0, V7X_LANES - RPB_COLS)))
    r = r.reshape(N_HEAD_GROUPS, HEADS_PER_GROUP, RPB_ROWS, V7X_LANES)
    w = HEADS_PER_GROUP * GRID_W
    return pl.pallas_call(
        _bias_kernel,
        out_shape=jax.ShapeDtypeStruct((N_HEAD_GROUPS, RPB_ROWS, GRID_W, w), F32),
        grid=(N_HEAD_GROUPS,),
        in_specs=[pl.BlockSpec((1, HEADS_PER_GROUP, RPB_ROWS, V7X_LANES),
                               lambda g: (g, 0, 0, 0))],
        out_specs=pl.BlockSpec((1, RPB_ROWS, GRID_W, w), lambda g: (g, 0, 0, 0)),
        compiler_params=_params(1),
        name="attn_bias_table",
    )(r)


def _key_slot_blocks(u, n_steps):
    first = u == 0
    last = u == n_steps - 1
    return (jnp.where(first, 1, 2 * u - 1),
            2 * u,
            2 * u + 1,
            jnp.where(first, 2, jnp.where(last, 2 * u, 2 * u + 2)))


def _attn_kernel(q_ref, k0_ref, k1_ref, k2_ref, k3_ref, v0_ref, v1_ref, v2_ref, v3_ref,
                 bias_ref, o_ref, s_scr, p_scr, psum_scr, o_scr):
    u = pl.program_id(1)
    k_refs = (k0_ref, k1_ref, k2_ref, k3_ref)
    v_refs = (v0_ref, v1_ref, v2_ref, v3_ref)
    w = V7X_MXU_DIM
    half = V7X_LANES
    slot_off = [KEY_BLOCK_ROWS * blk - Q_ROWS * u + WIN_H - 1
                for blk in _key_slot_blocks(u, pl.num_programs(1))]
    lane = lax.broadcasted_iota(jnp.int32, (GRID_W, half), 1)
    low_head = lane < HEAD_DIM
    keep_low = jnp.where(low_head, 1.0, 0.0).astype(BF16)
    keep_high = jnp.where(low_head, 0.0, 1.0).astype(BF16)
    zero_half = jnp.zeros((GRID_W, half), BF16)
    r_id = lax.broadcasted_iota(jnp.int32, (w, w), 0) // HEAD_DIM
    c_id = lax.broadcasted_iota(jnp.int32, (w, w), 1) // HEAD_DIM
    ones_bd = jnp.where(r_id == c_id, 1.0, 0.0).astype(BF16)
    users = [(j, max(0, j - WIN_H + 1), min(Q_ROWS, j + 1)) for j in range(WIN_TILES)]

    def tile(refs, j, lanes):
        slot, row = divmod(j, KEY_BLOCK_ROWS)
        t = refs[slot][0, row * GRID_W:(row + 1) * GRID_W, lanes]
        t01, t23 = t[:, :half], t[:, half:]
        return jnp.concatenate([
            jnp.concatenate([t01 * keep_low, zero_half], axis=1),
            jnp.concatenate([t01 * keep_high, zero_half], axis=1),
            jnp.concatenate([zero_half, t23 * keep_low], axis=1),
            jnp.concatenate([zero_half, t23 * keep_high], axis=1)], axis=0)

    def head_max(m1):
        cols = []
        for c in range(w // half):
            x = m1[:, c * half:(c + 1) * half]
            lo = jnp.max(jnp.where(low_head, x, NEG_INF), axis=-1, keepdims=True)
            hi = jnp.max(jnp.where(low_head, NEG_INF, x), axis=-1, keepdims=True)
            cols.append(jnp.where(low_head, lo, hi))
        return jnp.concatenate(cols, axis=-1)

    def scores(g, tiles):
        sl = g % N_ATTN_SLOTS
        lanes = slice(g * w, (g + 1) * w)
        for j, ilo, ihi in tiles:
            kbd = tile(k_refs, j, lanes)
            s = lax.dot_general(q_ref[0, ilo * GRID_W:ihi * GRID_W, lanes], kbd,
                                (((1,), (1,)), ((), ())), preferred_element_type=F32)
            for i in range(ilo, ihi):
                d = slot_off[j // KEY_BLOCK_ROWS] + (j % KEY_BLOCK_ROWS - i)
                s_scr[sl, i, j - i] = (s[(i - ilo) * GRID_W:(i - ilo + 1) * GRID_W]
                                       + bias_ref[g, d])

    def softmax_numerators(g, i):
        sl = g % N_ATTN_SLOTS
        m1 = s_scr[sl, i, 0]
        for jj in range(1, WIN_H):
            m1 = jnp.maximum(m1, s_scr[sl, i, jj])
        m = head_max(m1)
        psum = jnp.zeros((GRID_W, w), F32)
        for jj in range(WIN_H):
            p = jnp.exp2(s_scr[sl, i, jj] - m)
            psum = psum + p
            p_scr[sl, i + jj, i * GRID_W:(i + 1) * GRID_W, :] = p.astype(BF16)
        psum_scr[sl, i * GRID_W:(i + 1) * GRID_W, :] = psum

    def weighted_values(g, tiles):
        sl = g % N_ATTN_SLOTS
        lanes = slice(g * w, (g + 1) * w)
        for j, ilo, ihi in tiles:
            vbd = tile(v_refs, j, lanes)
            pv = _dot(p_scr[sl, j, ilo * GRID_W:ihi * GRID_W, :], vbd)
            for i in range(ilo, ihi):
                rows = slice(i * GRID_W, (i + 1) * GRID_W)
                part = pv[(i - ilo) * GRID_W:(i - ilo + 1) * GRID_W]
                if j == i:
                    o_scr[sl, rows, :] = part
                else:
                    o_scr[sl, rows, :] = o_scr[sl, rows, :] + part

    def normalise(g):
        sl = g % N_ATTN_SLOTS
        lanes = slice(g * w, (g + 1) * w)
        ps = psum_scr[sl]
        ps_hi = ps.astype(BF16)
        ps_lo = (ps - ps_hi.astype(F32)).astype(BF16)
        denom = _dot(ps_hi, ones_bd) + _dot(ps_lo, ones_bd)
        o_ref[0, :, lanes] = (o_scr[sl] / denom).astype(BF16)

    for g in range(min(N_ATTN_SLOTS, N_HEAD_GROUPS)):
        scores(g, users)
    for g in range(N_HEAD_GROUPS):
        ahead = g + N_ATTN_SLOTS if g + N_ATTN_SLOTS < N_HEAD_GROUPS else None
        for i in range(Q_ROWS):
            softmax_numerators(g, i)
            weighted_values(g, users[i:i + 1])
            if ahead is not None:
                scores(ahead, users[i:i + 1])
        weighted_values(g, users[Q_ROWS:])
        if ahead is not None:
            scores(ahead, users[Q_ROWS:])
        normalise(g)


def _attention(q, k, v, bias):
    bsz, t, d = q.shape
    assert t % Q_TOK == 0 and t // Q_TOK >= 2
    n_steps = t // Q_TOK
    w = V7X_MXU_DIM
    tok = pl.BlockSpec((1, Q_TOK, d), lambda b, u: (b, u, 0))

    def key_slot(slot):
        return pl.BlockSpec((1, KEY_BLOCK_TOK, d),
                            lambda b, u: (b, _key_slot_blocks(u, n_steps)[slot], 0))

    key_slots = [key_slot(s) for s in range(N_KEY_SLOTS)]
    return pl.pallas_call(
        _attn_kernel,
        out_shape=jax.ShapeDtypeStruct((bsz, t, d), BF16),
        grid=(bsz, n_steps),
        in_specs=[tok] + key_slots + key_slots
                 + [_resident(bias.shape)],
        out_specs=tok,
        scratch_shapes=[
            pltpu.VMEM((N_ATTN_SLOTS, Q_ROWS, WIN_H, GRID_W, w), F32),
            pltpu.VMEM((N_ATTN_SLOTS, WIN_TILES, Q_TOK, w), BF16),
            pltpu.VMEM((N_ATTN_SLOTS, Q_TOK, w), F32),
            pltpu.VMEM((N_ATTN_SLOTS, Q_TOK, w), F32),
        ],
        compiler_params=_params(2),
        name="neighbourhood_attention",
    )(q, *([k] * N_KEY_SLOTS), *([v] * N_KEY_SLOTS), bias)


def _halo_specs(t, d):
    tm = TOKEN_TILE
    per_tile = tm // HALO
    n_halo_blocks = t // HALO
    main = pl.BlockSpec((1, tm, d), lambda b, i: (b, i, 0))
    prev = pl.BlockSpec((1, HALO, d), lambda b, i: (b, jnp.maximum(i * per_tile - 1, 0), 0))
    nxt = pl.BlockSpec(
        (1, HALO, d), lambda b, i: (b, jnp.minimum((i + 1) * per_tile, n_halo_blocks - 1), 0))
    return main, prev, nxt


def _sub_tile_rows(prev_ref, main_ref, next_ref, s):
    lo = s * SUB_TILE
    hi = lo + SUB_TILE
    before = prev_ref[0] if s == 0 else main_ref[0, lo - HALO:lo, :]
    after = next_ref[0] if s == N_SUB - 1 else main_ref[0, hi:hi + HALO, :]
    return jnp.concatenate([before, main_ref[0, lo:hi, :], after], axis=0)


def _sub_tile_inside(s):
    lo_inside = True if s > 0 else pl.program_id(1) > 0
    hi_inside = True if s < N_SUB - 1 else pl.program_id(1) < pl.num_programs(1) - 1
    return lo_inside, hi_inside


def _run_interleaved(chains):
    live = list(chains)
    while live:
        for chain in list(live):
            try:
                next(chain)
            except StopIteration:
                live.remove(chain)


def _conv3(a, w, lo, n):
    rows = a.shape[0]
    prev = pltpu.roll(a, 1, 0)
    nxt = pltpu.roll(a, rows - 1, 0)
    return prev[lo:lo + n] * w[0:1] + a[lo:lo + n] * w[1:2] + nxt[lo:lo + n] * w[2:3]


def _conv_ffn_stages(x1, halo, inside, mod, g, wup_ref, cw_ref, cb_ref, wdn_ref, out):
    tm = SUB_TILE
    h2 = _modnorm(x1, g, mod[4:5], mod[3:4])
    h2 = _zero_outside_sequence(h2, halo, *inside)
    h2_ext = h2.astype(BF16)
    h2_main = h2[halo:halo + tm].astype(BF16)
    yield

    def up(c0, c1):
        return (_dot(h2_ext, wup_ref[0, :, c0:c1]),
                _dot(h2_main, wup_ref[0, :, D_FF + c0:D_FF + c1]))

    acc = None
    ahead = up(*FFN_CHUNKS[0])
    for c, (c0, c1) in enumerate(FFN_CHUNKS):
        a, val = ahead
        if c + 1 < len(FFN_CHUNKS):
            ahead = up(*FFN_CHUNKS[c + 1])
        yield
        a = _conv3(a, cw_ref[0, :, c0:c1], halo, tm) + cb_ref[0, :, c0:c1]
        gated = (_silu(a) * val).astype(BF16)
        yield
        part = _dot(gated, wdn_ref[0, c0:c1, :])
        acc = part if acc is None else acc + part
    out.append(acc)


def _attn_out_ffn_kernel(xm_ref, xp_ref, xn_ref, om_ref, op_ref, on_ref, mod_ref, g_ref,
                         wo_ref, wup_ref, cw_ref, cb_ref, wdn_ref, out_ref):
    mod = mod_ref[0, 0]

    def chain(s):
        x = _sub_tile_rows(xp_ref, xm_ref, xn_ref, s)
        o = _sub_tile_rows(op_ref, om_ref, on_ref, s)
        x1 = x + mod[2:3] * _dot(o, wo_ref[0])
        yield
        ffn = []
        yield from _conv_ffn_stages(x1, HALO, _sub_tile_inside(s), mod, g_ref[0],
                                    wup_ref, cw_ref, cb_ref, wdn_ref, ffn)
        out_ref[0, s * SUB_TILE:(s + 1) * SUB_TILE, :] = (
            x1[HALO:HALO + SUB_TILE] + mod[5:6] * ffn[0])

    _run_interleaved([chain(s) for s in range(N_SUB)])


def _attn_out_ffn(x, o, mod, seq0, layer, na_layer, g2, w_o, w_up, conv_w, conv_b, w_down):
    bsz, t, d = x.shape
    main, prev, nxt = _halo_specs(t, d)
    return pl.pallas_call(
        _attn_out_ffn_kernel,
        out_shape=jax.ShapeDtypeStruct((bsz, t, d), F32),
        grid=(bsz, t // TOKEN_TILE),
        in_specs=[
            main, prev, nxt, main, prev, nxt,
            pl.BlockSpec((1, 1, 6, d), lambda b, i: (layer, seq0 + b, 0, 0)),
            _layer_block(g2, layer),
            _layer_block(w_o, na_layer),
            _layer_block(w_up, layer),
            _layer_block(conv_w, layer),
            _layer_block(conv_b, layer),
            _layer_block(w_down, layer),
        ],
        out_specs=main,
        compiler_params=_params(2),
        name="attn_out_conv_ffn",
    )(x, x, x, o, o, o, mod, g2, w_o, w_up, conv_w, conv_b, w_down)


def _shortconv_layer_kernel(xm_ref, xp_ref, xn_ref, mod_ref, g1_ref, g2_ref, gf_ref,
                            win_ref, scw_ref, wout_ref, wup_ref, cw_ref, cb_ref, wdn_ref,
                            out_ref):
    tm = SUB_TILE
    d = D_MODEL
    inner = HALO // 2
    n1 = tm + 2 * inner
    lo = HALO - inner
    mod = mod_ref[0, 0]
    scw = scw_ref[0]

    def chain(s):
        inside = _sub_tile_inside(s)
        x = _sub_tile_rows(xp_ref, xm_ref, xn_ref, s)
        h1 = _modnorm(x, g1_ref[0], mod[1:2], mod[0:1]).astype(BF16)
        yield
        cg = _dot(h1, win_ref[0, :, d:2 * d])
        uu = _dot(h1, win_ref[0, :, 2 * d:])
        bg = _dot(h1, win_ref[0, :, 0:d])
        yield
        cu = _zero_outside_sequence(cg * uu, HALO, *inside)
        z = (bg[lo:lo + n1] * _conv3(cu, scw, lo, n1)).astype(BF16)
        yield
        x1 = x[lo:lo + n1] + mod[2:3] * _dot(z, wout_ref[0])
        yield
        ffn = []
        yield from _conv_ffn_stages(x1, inner, inside, mod, g2_ref[0],
                                    wup_ref, cw_ref, cb_ref, wdn_ref, ffn)
        x2 = x1[inner:inner + tm] + mod[5:6] * ffn[0]
        out_ref[0, s * tm:(s + 1) * tm, :] = _rmsnorm(x2, gf_ref[...])

    _run_interleaved([chain(s) for s in range(N_SUB)])


def _shortconv_layer(x, mod, seq0, layer, sc_layer, g1, g2, gf, w_in, sc_conv_w, w_out,
                     w_up, conv_w, conv_b, w_down):
    bsz, t, d = x.shape
    main, prev, nxt = _halo_specs(t, d)
    return pl.pallas_call(
        _shortconv_layer_kernel,
        out_shape=jax.ShapeDtypeStruct((bsz, t, d), F32),
        grid=(bsz, t // TOKEN_TILE),
        in_specs=[
            main, prev, nxt,
            pl.BlockSpec((1, 1, 6, d), lambda b, i: (layer, seq0 + b, 0, 0)),
            _layer_block(g1, layer), _layer_block(g2, layer), _resident((1, d)),
            _layer_block(w_in, sc_layer),
            _layer_block(sc_conv_w, sc_layer),
            _layer_block(w_out, sc_layer),
            _layer_block(w_up, layer),
            _layer_block(conv_w, layer),
            _layer_block(conv_b, layer),
            _layer_block(w_down, layer),
        ],
        out_specs=main,
        compiler_params=_params(2),
        name="shortconv_layer",
    )(x, x, x, mod, g1, g2, gf, w_in, sc_conv_w, w_out, w_up, conv_w, conv_b, w_down)


def _trunk(x, seq0, mod, p):
    assert x.shape[1] % TOKEN_TILE == 0 and x.shape[1] % QKV_TILE == 0
    q, k, v = _qkv_proj(x, mod, seq0, 0, 0, p["norm1_g"], p["w_qkv"], p["b_qkv"])
    o = _attention(q, k, v, p["attn_bias"])
    x = _attn_out_ffn(x, o, mod, seq0, 0, 0, p["norm2_g"], p["w_o"], p["w_up"],
                      p["ffn_conv_w"], p["ffn_conv_b"], p["w_down"])
    return _shortconv_layer(x, mod, seq0, 1, 0, p["norm1_g"], p["norm2_g"], p["final_g"],
                            p["w_in"], p["sc_conv_w"], p["w_out"], p["w_up"],
                            p["ffn_conv_w"], p["ffn_conv_b"], p["w_down"])


def kernel(x_prompt, x_sample, c_prompt, c_sample, ada_w, ada_b, norm1_g, norm2_g,
           na_w_qkv, na_b_qkv, na_rpb, na_w_o, sc_w_in, sc_conv_w, sc_w_out,
           ffn_w_up, ffn_conv_w, ffn_conv_b, ffn_w_down, final_g):
    depth = ada_w.shape[0]
    n_prompt = c_prompt.shape[0]
    n_seq = n_prompt + c_sample.shape[0]
    n_pad = -n_seq % 8
    c_all = jnp.concatenate([c_prompt, c_sample, jnp.zeros((n_pad, D_MODEL), F32)], axis=0)
    mod = _modulation(c_all, ada_w, ada_b).reshape(depth, n_seq + n_pad, 6, D_MODEL)
    p = dict(
        norm1_g=norm1_g.reshape(depth, 1, D_MODEL), norm2_g=norm2_g.reshape(depth, 1, D_MODEL),
        final_g=final_g.reshape(1, D_MODEL),
        w_qkv=na_w_qkv.astype(BF16), b_qkv=na_b_qkv.reshape(-1, 1, 3 * D_MODEL),
        attn_bias=_attn_bias_table(na_rpb[0]),
        w_o=na_w_o.astype(BF16),
        w_in=sc_w_in.astype(BF16), sc_conv_w=sc_conv_w, w_out=sc_w_out.astype(BF16),
        w_up=ffn_w_up.astype(BF16), ffn_conv_w=ffn_conv_w,
        ffn_conv_b=ffn_conv_b.reshape(depth, 1, D_FF),
        w_down=ffn_w_down.astype(BF16),
    )
    y_prompt = _trunk(x_prompt, 0, mod, p)
    y_sample = _trunk(x_sample, n_prompt, mod, p)
    return (y_prompt, y_sample)
```

```python
import math

import numpy as np
import jax
import jax.numpy as jnp
from jax import lax
from jax.experimental import pallas as pl
from jax.experimental.pallas import tpu as pltpu

F32 = jnp.float32
BF16 = jnp.bfloat16

D_MODEL = 1024
N_HEADS = 16
HEAD_DIM = D_MODEL // N_HEADS
GRID_W = 64
WIN_H = 8
WIN_W = 16
RPB_ROWS = 2 * WIN_H - 1
RPB_COLS = 2 * WIN_W - 1
D_FF = 2816
EPS = 1e-6
NEG_INF = -1e30
LOG2E = math.log2(math.e)

V7X_LANES = 128
V7X_MXU_DIM = 256
V7X_VMEM_LIMIT_BYTES = 56 * 1024 * 1024

HEADS_PER_GROUP = V7X_MXU_DIM // HEAD_DIM
N_HEAD_GROUPS = N_HEADS // HEADS_PER_GROUP
Q_ROWS = 8
Q_TOK = Q_ROWS * GRID_W
KEY_BLOCK_ROWS = 4
KEY_BLOCK_TOK = KEY_BLOCK_ROWS * GRID_W
N_KEY_SLOTS = (Q_ROWS + WIN_H) // KEY_BLOCK_ROWS
WIN_TILES = Q_ROWS + WIN_H - 1
N_ATTN_SLOTS = 2
KEY_COL_BLOCK = V7X_LANES // HEADS_PER_GROUP
N_KEY_COL_BLOCKS = GRID_W // KEY_COL_BLOCK
SLAB_ROWS = 40
SLAB_START = (0, GRID_W - SLAB_ROWS)
P_SLAB_ROWS = 48
P_SLAB_START = (0, GRID_W - P_SLAB_ROWS)

QKV_TILE = 1024
TOKEN_TILE = 512
SUB_TILE = 512
N_SUB = TOKEN_TILE // SUB_TILE
HALO = 16
F32_HALO = 8
FFN_CHUNKS = ((0, 1536), (1536, D_FF))


def _dot(a, b):
    return jnp.dot(a, b, preferred_element_type=F32)


def _silu(a):
    h = 0.5 * a
    return h + h * jnp.tanh(h)


def _rmsnorm(x, g):
    ms = jnp.mean(x * x, axis=-1, keepdims=True)
    return x * lax.rsqrt(ms + EPS) * g


def _modnorm(x, g, scale, shift):
    return _rmsnorm(x, g * (1.0 + scale)) + shift


def _zero_outside_sequence(v, halo, lo_inside, hi_inside):
    if lo_inside is True and hi_inside is True:
        return v
    n_rows = v.shape[0]
    row = lax.broadcasted_iota(jnp.int32, (n_rows, 1), 0)
    lo_ok = 0 if lo_inside is True else jnp.where(lo_inside, 0, halo)
    hi_ok = n_rows if hi_inside is True else jnp.where(hi_inside, n_rows, n_rows - halo)
    return jnp.where((row >= lo_ok) & (row < hi_ok), v, 0.0)


def _resident(shape, index=None):
    index = (0,) * len(shape) if index is None else index
    return pl.BlockSpec(shape, lambda *_: index, pipeline_mode=pl.Buffered(1))


def _layer_block(arr, layer):
    return _resident((1,) + arr.shape[1:], (layer,) + (0,) * (arr.ndim - 1))


def _params(dims):
    return pltpu.CompilerParams(dimension_semantics=("parallel",) * dims,
                                vmem_limit_bytes=V7X_VMEM_LIMIT_BYTES)


def _mod_kernel(c_ref, w_ref, b_ref, o_ref):
    s = _silu(c_ref[...])
    s_hi = s.astype(BF16)
    s_lo = (s - s_hi.astype(F32)).astype(BF16)
    w = w_ref[0]
    w_hi = w.astype(BF16)
    w_lo = (w - w_hi.astype(F32)).astype(BF16)
    acc = _dot(s_hi, w_hi) + _dot(s_lo, w_hi) + _dot(s_hi, w_lo)
    o_ref[0] = acc + b_ref[0]


def _modulation(c_all, ada_w, ada_b):
    depth = ada_w.shape[0]
    n_seq = c_all.shape[0]
    cn = D_MODEL
    return pl.pallas_call(
        _mod_kernel,
        out_shape=jax.ShapeDtypeStruct((depth, n_seq, 6 * D_MODEL), F32),
        grid=(depth, 6 * D_MODEL // cn),
        in_specs=[
            pl.BlockSpec((n_seq, D_MODEL), lambda l, n: (0, 0)),
            pl.BlockSpec((1, D_MODEL, cn), lambda l, n: (l, 0, n)),
            pl.BlockSpec((1, 1, cn), lambda l, n: (l, 0, n)),
        ],
        out_specs=pl.BlockSpec((1, n_seq, cn), lambda l, n: (l, 0, n)),
        compiler_params=_params(2),
        name="adaln_modulation",
    )(c_all, ada_w, ada_b.reshape(depth, 1, 6 * D_MODEL))


def _qkv_kernel(x_ref, mod_ref, g_ref, w_ref, b_ref, q_ref, k_ref, v_ref):
    mod = mod_ref[0, 0]
    h = _modnorm(x_ref[0], g_ref[0], mod[1:2], mod[0:1]).astype(BF16)
    d = D_MODEL
    q = (_dot(h, w_ref[0, :, 0:d]) + b_ref[0, :, 0:d]) * (HEAD_DIM ** -0.5 * LOG2E)
    q_ref[0] = q.astype(BF16)
    k_ref[0] = (_dot(h, w_ref[0, :, d:2 * d]) + b_ref[0, :, d:2 * d]).astype(BF16)
    v_ref[0] = (_dot(h, w_ref[0, :, 2 * d:]) + b_ref[0, :, 2 * d:]).astype(BF16)


def _qkv_proj(x, mod, seq0, layer, na_layer, g, w_qkv, b_qkv):
    bsz, t, d = x.shape
    tm = QKV_TILE
    tok = pl.BlockSpec((1, tm, d), lambda b, i: (b, i, 0))
    out = jax.ShapeDtypeStruct((bsz, t, d), BF16)
    return pl.pallas_call(
        _qkv_kernel,
        out_shape=(out, out, out),
        grid=(bsz, t // tm),
        in_specs=[
            tok,
            pl.BlockSpec((1, 1, 6, d), lambda b, i: (layer, seq0 + b, 0, 0)),
            _layer_block(g, layer),
            _layer_block(w_qkv, na_layer),
            _layer_block(b_qkv, na_layer),
        ],
        out_specs=(tok, tok, tok),
        compiler_params=_params(2),
        name="norm_qkv",
    )(x, mod, g, w_qkv, b_qkv)


def _bias_kernel(rpb_ref, o_ref):
    shape = (GRID_W, V7X_LANES)
    qc = lax.broadcasted_iota(jnp.int32, shape, 0)
    lane = lax.broadcasted_iota(jnp.int32, shape, 1)
    head = lane // KEY_COL_BLOCK
    start = jnp.clip(qc - WIN_W // 2, 0, GRID_W - WIN_W)
    for d in range(RPB_ROWS):
        for kb in range(N_KEY_COL_BLOCKS):
            kc = kb * KEY_COL_BLOCK + (lane & (KEY_COL_BLOCK - 1))
            valid = (kc >= start) & (kc < start + WIN_W)
            b = None
            for h in range(HEADS_PER_GROUP):
                r = jnp.broadcast_to(rpb_ref[0, h, d:d + 1, :], shape)
                shift = (KEY_COL_BLOCK * (h - kb) - (WIN_W - 1)) % V7X_LANES
                t = pltpu.roll(r, shift, 1, stride=1, stride_axis=0)
                b = t if b is None else jnp.where(head == h, t, b)
            b = jnp.where(valid, b * LOG2E, NEG_INF)
            lo = SLAB_START[kb]
            o_ref[0, d, kb] = b[lo:lo + SLAB_ROWS]


def _attn_bias_table(rpb):
    r = jnp.pad(rpb.astype(F32), ((0, 0), (0, 0), (0, V7X_LANES - RPB_COLS)))
    r = r.reshape(N_HEAD_GROUPS, HEADS_PER_GROUP, RPB_ROWS, V7X_LANES)
    blk = (1, RPB_ROWS, N_KEY_COL_BLOCKS, SLAB_ROWS, V7X_LANES)
    return pl.pallas_call(
        _bias_kernel,
        out_shape=jax.ShapeDtypeStruct((N_HEAD_GROUPS,) + blk[1:], F32),
        grid=(N_HEAD_GROUPS,),
        in_specs=[pl.BlockSpec((1, HEADS_PER_GROUP, RPB_ROWS, V7X_LANES),
                               lambda g: (g, 0, 0, 0))],
        out_specs=pl.BlockSpec(blk, lambda g: (g, 0, 0, 0, 0)),
        compiler_params=_params(1),
        name="attn_bias_table",
    )(r)


def _key_slot_blocks(u, n_steps):
    first = u == 0
    last = u == n_steps - 1
    return (jnp.where(first, 1, 2 * u - 1),
            2 * u,
            2 * u + 1,
            jnp.where(first, 2, jnp.where(last, 2 * u, 2 * u + 2)))


def _attn_kernel(q_ref, k0_ref, k1_ref, k2_ref, k3_ref, v0_ref, v1_ref, v2_ref, v3_ref,
                 bias_ref, o_ref, s_scr, p_scr, psum_scr, o_scr):
    u = pl.program_id(1)
    k_refs = (k0_ref, k1_ref, k2_ref, k3_ref)
    v_refs = (v0_ref, v1_ref, v2_ref, v3_ref)
    w = V7X_MXU_DIM
    half = V7X_LANES
    kcb = KEY_COL_BLOCK
    slot_off = [KEY_BLOCK_ROWS * blk - Q_ROWS * u + WIN_H - 1
                for blk in _key_slot_blocks(u, pl.num_programs(1))]
    feat_lane = lax.broadcasted_iota(jnp.int32, (kcb, half), 1)
    keep_low = jnp.where(feat_lane < HEAD_DIM, 1.0, 0.0).astype(BF16)
    keep_high = jnp.where(feat_lane < HEAD_DIM, 0.0, 1.0).astype(BF16)
    zero_half = jnp.zeros((kcb, half), BF16)
    col_head = lax.broadcasted_iota(jnp.int32, (GRID_W, half), 1) // kcb
    r_head = (lax.broadcasted_iota(jnp.int32, (w, w), 0) % half) // kcb
    c_head = lax.broadcasted_iota(jnp.int32, (w, w), 1) // HEAD_DIM
    ones_bd = jnp.where(r_head == c_head, 1.0, 0.0).astype(BF16)
    users = [(j, max(0, j - WIN_H + 1), min(Q_ROWS, j + 1)) for j in range(WIN_TILES)]

    @pl.when((pl.program_id(0) == 0) & (u == 0))
    def _():
        p_scr[...] = jnp.zeros(p_scr.shape, BF16)

    def tile(refs, j, lanes):
        slot, row = divmod(j, KEY_BLOCK_ROWS)
        t = refs[slot][0, row * GRID_W:(row + 1) * GRID_W, lanes]
        pieces = []
        for kb in range(N_KEY_COL_BLOCKS):
            tk = t[kb * kcb:(kb + 1) * kcb]
            t01, t23 = tk[:, :half], tk[:, half:]
            pieces += [jnp.concatenate([t01 * keep_low, zero_half], axis=1),
                       jnp.concatenate([t01 * keep_high, zero_half], axis=1),
                       jnp.concatenate([zero_half, t23 * keep_low], axis=1),
                       jnp.concatenate([zero_half, t23 * keep_high], axis=1)]
        return jnp.concatenate(pieces, axis=0)

    def head_max(m1):
        out = None
        for h in range(HEADS_PER_GROUP):
            mh = jnp.max(jnp.where(col_head == h, m1, NEG_INF), axis=-1, keepdims=True)
            out = mh if out is None else jnp.where(col_head == h, mh, out)
        return out

    def scores(g, tiles):
        sl = g % N_ATTN_SLOTS
        lanes = slice(g * w, (g + 1) * w)
        for j, ilo, ihi in tiles:
            kbd = tile(k_refs, j, lanes)
            s = lax.dot_general(q_ref[0, ilo * GRID_W:ihi * GRID_W, lanes], kbd,
                                (((1,), (1,)), ((), ())), preferred_element_type=F32)
            for i in range(ilo, ihi):
                d = slot_off[j // KEY_BLOCK_ROWS] + (j % KEY_BLOCK_ROWS - i)
                r0 = (i - ilo) * GRID_W
                for kb in range(N_KEY_COL_BLOCKS):
                    lo = r0 + SLAB_START[kb]
                    s_scr[sl, i, j - i, kb] = (s[lo:lo + SLAB_ROWS, kb * half:(kb + 1) * half]
                                               + bias_ref[g, d, kb])

    def softmax_numerators(g, i):
        sl = g % N_ATTN_SLOTS
        blocks = range(N_KEY_COL_BLOCKS)
        m1 = [s_scr[sl, i, 0, kb] for kb in blocks]
        for jj in range(1, WIN_H):
            m1 = [jnp.maximum(m1[kb], s_scr[sl, i, jj, kb]) for kb in blocks]
        ov = SLAB_ROWS - SLAB_START[1]
        m_rows = jnp.concatenate([m1[0][:SLAB_START[1]],
                                  jnp.maximum(m1[0][SLAB_START[1]:], m1[1][:ov]),
                                  m1[1][ov:]], axis=0)
        m = head_max(m_rows)
        ms = [m[SLAB_START[kb]:SLAB_START[kb] + SLAB_ROWS] for kb in blocks]
        psum = [jnp.zeros((SLAB_ROWS, half), F32) for _ in blocks]
        pad = jnp.zeros((P_SLAB_ROWS - SLAB_ROWS, half), F32)
        r0 = i * GRID_W
        for jj in range(WIN_H):
            for kb in blocks:
                p = jnp.exp2(s_scr[sl, i, jj, kb] - ms[kb])
                psum[kb] = psum[kb] + p
                wide = jnp.concatenate([p, pad] if kb == 0 else [pad, p], axis=0)
                lo = r0 + P_SLAB_START[kb]
                p_scr[sl, i + jj, lo:lo + P_SLAB_ROWS, kb * half:(kb + 1) * half] = (
                    wide.astype(BF16))
        rest = jnp.zeros((GRID_W - SLAB_ROWS, half), F32)
        psum_scr[sl, r0:r0 + GRID_W, 0:half] = jnp.concatenate([psum[0], rest], axis=0)
        psum_scr[sl, r0:r0 + GRID_W, half:w] = jnp.concatenate([rest, psum[1]], axis=0)

    def weighted_values(g, tiles):
        sl = g % N_ATTN_SLOTS
        lanes = slice(g * w, (g + 1) * w)
        for j, ilo, ihi in tiles:
            vbd = tile(v_refs, j, lanes)
            pv = _dot(p_scr[sl, j, ilo * GRID_W:ihi * GRID_W, :], vbd)
            for i in range(ilo, ihi):
                rows = slice(i * GRID_W, (i + 1) * GRID_W)
                part = pv[(i - ilo) * GRID_W:(i - ilo + 1) * GRID_W]
                if j == i:
                    o_scr[sl, rows, :] = part
                else:
                    o_scr[sl, rows, :] = o_scr[sl, rows, :] + part

    def normalise(g):
        sl = g % N_ATTN_SLOTS
        lanes = slice(g * w, (g + 1) * w)
        ps = psum_scr[sl]
        ps_hi = ps.astype(BF16)
        ps_lo = (ps - ps_hi.astype(F32)).astype(BF16)
        denom = _dot(ps_hi, ones_bd) + _dot(ps_lo, ones_bd)
        o_ref[0, :, lanes] = (o_scr[sl] / denom).astype(BF16)

    for g in range(min(N_ATTN_SLOTS, N_HEAD_GROUPS)):
        scores(g, users)
    for g in range(N_HEAD_GROUPS):
        ahead = g + N_ATTN_SLOTS if g + N_ATTN_SLOTS < N_HEAD_GROUPS else None
        for i in range(Q_ROWS):
            softmax_numerators(g, i)
            weighted_values(g, users[i:i + 1])
            if ahead is not None:
                scores(ahead, users[i:i + 1])
        weighted_values(g, users[Q_ROWS:])
        if ahead is not None:
            scores(ahead, users[Q_ROWS:])
        normalise(g)


def _attention(q, k, v, bias):
    bsz, t, d = q.shape
    assert t % Q_TOK == 0 and t // Q_TOK >= 2
    n_steps = t // Q_TOK
    w = V7X_MXU_DIM
    tok = pl.BlockSpec((1, Q_TOK, d), lambda b, u: (b, u, 0))

    def key_slot(slot):
        return pl.BlockSpec((1, KEY_BLOCK_TOK, d),
                            lambda b, u: (b, _key_slot_blocks(u, n_steps)[slot], 0))

    key_slots = [key_slot(s) for s in range(N_KEY_SLOTS)]
    return pl.pallas_call(
        _attn_kernel,
        out_shape=jax.ShapeDtypeStruct((bsz, t, d), BF16),
        grid=(bsz, n_steps),
        in_specs=[tok] + key_slots + key_slots
                 + [_resident(bias.shape)],
        out_specs=tok,
        scratch_shapes=[
            pltpu.VMEM((N_ATTN_SLOTS, Q_ROWS, WIN_H, N_KEY_COL_BLOCKS, SLAB_ROWS, V7X_LANES),
                       F32),
            pltpu.VMEM((N_ATTN_SLOTS, WIN_TILES, Q_TOK, w), BF16),
            pltpu.VMEM((N_ATTN_SLOTS, Q_TOK, w), F32),
            pltpu.VMEM((N_ATTN_SLOTS, Q_TOK, w), F32),
        ],
        compiler_params=pltpu.CompilerParams(dimension_semantics=("arbitrary", "arbitrary"),
                                             vmem_limit_bytes=V7X_VMEM_LIMIT_BYTES),
        name="neighbourhood_attention",
    )(q, *([k] * N_KEY_SLOTS), *([v] * N_KEY_SLOTS), bias)


def _halo_specs(t, d, halo=HALO):
    tm = TOKEN_TILE
    per_tile = tm // halo
    n_halo_blocks = t // halo
    main = pl.BlockSpec((1, tm, d), lambda b, i: (b, i, 0))
    prev = pl.BlockSpec((1, halo, d), lambda b, i: (b, jnp.maximum(i * per_tile - 1, 0), 0))
    nxt = pl.BlockSpec(
        (1, halo, d), lambda b, i: (b, jnp.minimum((i + 1) * per_tile, n_halo_blocks - 1), 0))
    return main, prev, nxt


def _sub_tile_rows(prev_ref, main_ref, next_ref, s, halo=HALO):
    lo = s * SUB_TILE
    hi = lo + SUB_TILE
    before = prev_ref[0] if s == 0 else main_ref[0, lo - halo:lo, :]
    after = next_ref[0] if s == N_SUB - 1 else main_ref[0, hi:hi + halo, :]
    return jnp.concatenate([before, main_ref[0, lo:hi, :], after], axis=0)


def _sub_tile_inside(s):
    lo_inside = True if s > 0 else pl.program_id(1) > 0
    hi_inside = True if s < N_SUB - 1 else pl.program_id(1) < pl.num_programs(1) - 1
    return lo_inside, hi_inside


def _run_interleaved(chains):
    live = list(chains)
    while live:
        for chain in list(live):
            try:
                next(chain)
            except StopIteration:
                live.remove(chain)


def _conv3(a, w, lo, n):
    rows = a.shape[0]
    prev = pltpu.roll(a, 1, 0)
    nxt = pltpu.roll(a, rows - 1, 0)
    return prev[lo:lo + n] * w[0:1] + a[lo:lo + n] * w[1:2] + nxt[lo:lo + n] * w[2:3]


def _conv_ffn_stages(x1, halo, inside, mod, g, wup_ref, cw_ref, cb_ref, wdn_ref, out):
    tm = SUB_TILE
    h2 = _modnorm(x1, g, mod[4:5], mod[3:4])
    h2 = _zero_outside_sequence(h2, halo, *inside)
    h2_ext = h2.astype(BF16)
    h2_main = h2[halo:halo + tm].astype(BF16)
    yield

    def up(c0, c1):
        return (_dot(h2_ext, wup_ref[0, :, c0:c1]),
                _dot(h2_main, wup_ref[0, :, D_FF + c0:D_FF + c1]))

    acc = None
    ahead = up(*FFN_CHUNKS[0])
    for c, (c0, c1) in enumerate(FFN_CHUNKS):
        a, val = ahead
        if c + 1 < len(FFN_CHUNKS):
            ahead = up(*FFN_CHUNKS[c + 1])
        yield
        a = _conv3(a, cw_ref[0, :, c0:c1], halo, tm) + cb_ref[0, :, c0:c1]
        gated = (_silu(a) * val).astype(BF16)
        yield
        part = _dot(gated, wdn_ref[0, c0:c1, :])
        acc = part if acc is None else acc + part
    out.append(acc)


def _attn_out_ffn_kernel(xm_ref, xp_ref, xn_ref, om_ref, op_ref, on_ref, mod_ref, g_ref,
                         wo_ref, wup_ref, cw_ref, cb_ref, wdn_ref, out_ref):
    mod = mod_ref[0, 0]

    def chain(s):
        x = _sub_tile_rows(xp_ref, xm_ref, xn_ref, s)
        o = _sub_tile_rows(op_ref, om_ref, on_ref, s)
        x1 = x + mod[2:3] * _dot(o, wo_ref[0])
        yield
        ffn = []
        yield from _conv_ffn_stages(x1, HALO, _sub_tile_inside(s), mod, g_ref[0],
                                    wup_ref, cw_ref, cb_ref, wdn_ref, ffn)
        out_ref[0, s * SUB_TILE:(s + 1) * SUB_TILE, :] = (
            x1[HALO:HALO + SUB_TILE] + mod[5:6] * ffn[0])

    _run_interleaved([chain(s) for s in range(N_SUB)])


def _attn_out_ffn(x, o, mod, seq0, layer, na_layer, g2, w_o, w_up, conv_w, conv_b, w_down):
    bsz, t, d = x.shape
    main, prev, nxt = _halo_specs(t, d)
    return pl.pallas_call(
        _attn_out_ffn_kernel,
        out_shape=jax.ShapeDtypeStruct((bsz, t, d), F32),
        grid=(bsz, t // TOKEN_TILE),
        in_specs=[
            main, prev, nxt, main, prev, nxt,
            pl.BlockSpec((1, 1, 6, d), lambda b, i: (layer, seq0 + b, 0, 0)),
            _layer_block(g2, layer),
            _layer_block(w_o, na_layer),
            _layer_block(w_up, layer),
            _layer_block(conv_w, layer),
            _layer_block(conv_b, layer),
            _layer_block(w_down, layer),
        ],
        out_specs=main,
        compiler_params=_params(2),
        name="attn_out_conv_ffn",
    )(x, x, x, o, o, o, mod, g2, w_o, w_up, conv_w, conv_b, w_down)


def _shortconv_layer_kernel(xm_ref, xp_ref, xn_ref, mod_ref, g1_ref, g2_ref, gf_ref,
                            win_ref, scw_ref, wout_ref, wup_ref, cw_ref, cb_ref, wdn_ref,
                            out_ref):
    tm = SUB_TILE
    d = D_MODEL
    halo = F32_HALO
    n_rows = tm + 2 * halo
    mod = mod_ref[0, 0]
    scw = scw_ref[0]

    def chain(s):
        inside = _sub_tile_inside(s)
        x = _sub_tile_rows(xp_ref, xm_ref, xn_ref, s, halo)
        h1 = _modnorm(x, g1_ref[0], mod[1:2], mod[0:1]).astype(BF16)
        yield
        cg = _dot(h1, win_ref[0, :, d:2 * d])
        uu = _dot(h1, win_ref[0, :, 2 * d:])
        bg = _dot(h1, win_ref[0, :, 0:d])
        yield
        cu = _zero_outside_sequence(cg * uu, halo, *inside)
        z = (bg * _conv3(cu, scw, 0, n_rows)).astype(BF16)
        yield
        x1 = x + mod[2:3] * _dot(z, wout_ref[0])
        yield
        ffn = []
        yield from _conv_ffn_stages(x1, halo, inside, mod, g2_ref[0],
                                    wup_ref, cw_ref, cb_ref, wdn_ref, ffn)
        x2 = x1[halo:halo + tm] + mod[5:6] * ffn[0]
        out_ref[0, s * tm:(s + 1) * tm, :] = _rmsnorm(x2, gf_ref[...])

    _run_interleaved([chain(s) for s in range(N_SUB)])


def _shortconv_layer(x, mod, seq0, layer, sc_layer, g1, g2, gf, w_in, sc_conv_w, w_out,
                     w_up, conv_w, conv_b, w_down):
    bsz, t, d = x.shape
    main, prev, nxt = _halo_specs(t, d, F32_HALO)
    return pl.pallas_call(
        _shortconv_layer_kernel,
        out_shape=jax.ShapeDtypeStruct((bsz, t, d), F32),
        grid=(bsz, t // TOKEN_TILE),
        in_specs=[
            main, prev, nxt,
            pl.BlockSpec((1, 1, 6, d), lambda b, i: (layer, seq0 + b, 0, 0)),
            _layer_block(g1, layer), _layer_block(g2, layer), _resident((1, d)),
            _layer_block(w_in, sc_layer),
            _layer_block(sc_conv_w, sc_layer),
            _layer_block(w_out, sc_layer),
            _layer_block(w_up, layer),
            _layer_block(conv_w, layer),
            _layer_block(conv_b, layer),
            _layer_block(w_down, layer),
        ],
        out_specs=main,
        compiler_params=_params(2),
        name="shortconv_layer",
    )(x, x, x, mod, g1, g2, gf, w_in, sc_conv_w, w_out, w_up, conv_w, conv_b, w_down)


def _trunk(x, seq0, mod, p):
    assert x.shape[1] % TOKEN_TILE == 0 and x.shape[1] % QKV_TILE == 0
    q, k, v = _qkv_proj(x, mod, seq0, 0, 0, p["norm1_g"], p["w_qkv"], p["b_qkv"])
    o = _attention(q, k, v, p["attn_bias"])
    x = _attn_out_ffn(x, o, mod, seq0, 0, 0, p["norm2_g"], p["w_o"], p["w_up"],
                      p["ffn_conv_w"], p["ffn_conv_b"], p["w_down"])
    return _shortconv_layer(x, mod, seq0, 1, 0, p["norm1_g"], p["norm2_g"], p["final_g"],
                            p["w_in"], p["sc_conv_w"], p["w_out"], p["w_up"],
                            p["ffn_conv_w"], p["ffn_conv_b"], p["w_down"])


def kernel(x_prompt, x_sample, c_prompt, c_sample, ada_w, ada_b, norm1_g, norm2_g,
           na_w_qkv, na_b_qkv, na_rpb, na_w_o, sc_w_in, sc_conv_w, sc_w_out,
           ffn_w_up, ffn_conv_w, ffn_conv_b, ffn_w_down, final_g):
    depth = ada_w.shape[0]
    n_prompt = c_prompt.shape[0]
    n_seq = n_prompt + c_sample.shape[0]
    n_pad = -n_seq % 8
    c_all = jnp.concatenate([c_prompt, c_sample, jnp.zeros((n_pad, D_MODEL), F32)], axis=0)
    mod = _modulation(c_all, ada_w, ada_b).reshape(depth, n_seq + n_pad, 6, D_MODEL)
    p = dict(
        norm1_g=norm1_g.reshape(depth, 1, D_MODEL), norm2_g=norm2_g.reshape(depth, 1, D_MODEL),
        final_g=final_g.reshape(1, D_MODEL),
        w_qkv=na_w_qkv.astype(BF16), b_qkv=na_b_qkv.reshape(-1, 1, 3 * D_MODEL),
        attn_bias=_attn_bias_table(na_rpb[0]),
        w_o=na_w_o.astype(BF16),
        w_in=sc_w_in.astype(BF16), sc_conv_w=sc_conv_w, w_out=sc_w_out.astype(BF16),
        w_up=ffn_w_up.astype(BF16), ffn_conv_w=ffn_conv_w,
        ffn_conv_b=ffn_conv_b.reshape(depth, 1, D_FF),
        w_down=ffn_w_down.astype(BF16),
    )
    y_prompt = _trunk(x_prompt, 0, mod, p)
    y_sample = _trunk(x_sample, n_prompt, mod, p)
    return (y_prompt, y_sample)
```

```python
import math

import numpy as np
import jax
import jax.numpy as jnp
from jax import lax
from jax.experimental import pallas as pl
from jax.experimental.pallas import tpu as pltpu

F32 = jnp.float32
BF16 = jnp.bfloat16

D_MODEL = 1024
N_HEADS = 16
HEAD_DIM = D_MODEL // N_HEADS
GRID_W = 64
WIN_H = 8
WIN_W = 16
RPB_ROWS = 2 * WIN_H - 1
RPB_COLS = 2 * WIN_W - 1
D_FF = 2816
EPS = 1e-6
NEG_INF = -1e30
LOG2E = math.log2(math.e)

V7X_LANES = 128
V7X_MXU_DIM = 256
V7X_VMEM_LIMIT_BYTES = 56 * 1024 * 1024

HEADS_PER_GROUP = V7X_MXU_DIM // HEAD_DIM
N_HEAD_GROUPS = N_HEADS // HEADS_PER_GROUP
Q_ROWS = 8
Q_TOK = Q_ROWS * GRID_W
KEY_BLOCK_ROWS = 4
KEY_BLOCK_TOK = KEY_BLOCK_ROWS * GRID_W
N_KEY_SLOTS = (Q_ROWS + WIN_H) // KEY_BLOCK_ROWS
WIN_TILES = Q_ROWS + WIN_H - 1
N_ATTN_SLOTS = 2
KEY_COL_BLOCK = V7X_LANES // HEADS_PER_GROUP
N_KEY_COL_BLOCKS = GRID_W // KEY_COL_BLOCK
SLAB_ROWS = 40
SLAB_START = (0, GRID_W - SLAB_ROWS)
P_SLAB_ROWS = 48
P_SLAB_START = (0, GRID_W - P_SLAB_ROWS)

QKV_TILE = 1024
TOKEN_TILE = 512
SUB_TILE = 512
N_SUB = TOKEN_TILE // SUB_TILE
HALO = 16
F32_HALO = 8
FFN_CHUNKS = ((0, 1536), (1536, D_FF))


def _dot(a, b):
    return jnp.dot(a, b, preferred_element_type=F32)


def _silu(a):
    h = 0.5 * a
    return h + h * jnp.tanh(h)


def _rmsnorm(x, g):
    ms = jnp.mean(x * x, axis=-1, keepdims=True)
    return x * lax.rsqrt(ms + EPS) * g


def _modnorm(x, g, scale, shift):
    return _rmsnorm(x, g * (1.0 + scale)) + shift


def _zero_outside_sequence(v, halo, lo_inside, hi_inside):
    if lo_inside is True and hi_inside is True:
        return v
    n_rows = v.shape[0]
    row = lax.broadcasted_iota(jnp.int32, (n_rows, 1), 0)
    lo_ok = 0 if lo_inside is True else jnp.where(lo_inside, 0, halo)
    hi_ok = n_rows if hi_inside is True else jnp.where(hi_inside, n_rows, n_rows - halo)
    return jnp.where((row >= lo_ok) & (row < hi_ok), v, 0.0)


def _resident(shape, index=None):
    index = (0,) * len(shape) if index is None else index
    return pl.BlockSpec(shape, lambda *_: index, pipeline_mode=pl.Buffered(1))


def _layer_block(arr, layer):
    return _resident((1,) + arr.shape[1:], (layer,) + (0,) * (arr.ndim - 1))


def _params(dims):
    return pltpu.CompilerParams(dimension_semantics=("parallel",) * dims,
                                vmem_limit_bytes=V7X_VMEM_LIMIT_BYTES)


def _mod_kernel(c_ref, w_ref, b_ref, o_ref):
    s = _silu(c_ref[...])
    s_hi = s.astype(BF16)
    s_lo = (s - s_hi.astype(F32)).astype(BF16)
    w = w_ref[0]
    w_hi = w.astype(BF16)
    w_lo = (w - w_hi.astype(F32)).astype(BF16)
    acc = _dot(s_hi, w_hi) + _dot(s_lo, w_hi) + _dot(s_hi, w_lo)
    o_ref[0] = acc + b_ref[0]


def _modulation(c_all, ada_w, ada_b):
    depth = ada_w.shape[0]
    n_seq = c_all.shape[0]
    cn = D_MODEL
    return pl.pallas_call(
        _mod_kernel,
        out_shape=jax.ShapeDtypeStruct((depth, n_seq, 6 * D_MODEL), F32),
        grid=(depth, 6 * D_MODEL // cn),
        in_specs=[
            pl.BlockSpec((n_seq, D_MODEL), lambda l, n: (0, 0)),
            pl.BlockSpec((1, D_MODEL, cn), lambda l, n: (l, 0, n)),
            pl.BlockSpec((1, 1, cn), lambda l, n: (l, 0, n)),
        ],
        out_specs=pl.BlockSpec((1, n_seq, cn), lambda l, n: (l, 0, n)),
        compiler_params=_params(2),
        name="adaln_modulation",
    )(c_all, ada_w, ada_b.reshape(depth, 1, 6 * D_MODEL))


def _qkv_kernel(x_ref, mod_ref, g_ref, w_ref, b_ref, q_ref, k_ref, v_ref):
    mod = mod_ref[0, 0]
    h = _modnorm(x_ref[0], g_ref[0], mod[1:2], mod[0:1]).astype(BF16)
    d = D_MODEL
    q = (_dot(h, w_ref[0, :, 0:d]) + b_ref[0, :, 0:d]) * (HEAD_DIM ** -0.5 * LOG2E)
    q_ref[0] = q.astype(BF16)
    k_ref[0] = (_dot(h, w_ref[0, :, d:2 * d]) + b_ref[0, :, d:2 * d]).astype(BF16)
    v_ref[0] = (_dot(h, w_ref[0, :, 2 * d:]) + b_ref[0, :, 2 * d:]).astype(BF16)


def _qkv_proj(x, mod, seq0, layer, na_layer, g, w_qkv, b_qkv):
    bsz, t, d = x.shape
    tm = QKV_TILE
    tok = pl.BlockSpec((1, tm, d), lambda b, i: (b, i, 0))
    out = jax.ShapeDtypeStruct((bsz, t, d), BF16)
    return pl.pallas_call(
        _qkv_kernel,
        out_shape=(out, out, out),
        grid=(bsz, t // tm),
        in_specs=[
            tok,
            pl.BlockSpec((1, 1, 6, d), lambda b, i: (layer, seq0 + b, 0, 0)),
            _layer_block(g, layer),
            _layer_block(w_qkv, na_layer),
            _layer_block(b_qkv, na_layer),
        ],
        out_specs=(tok, tok, tok),
        compiler_params=_params(2),
        name="norm_qkv",
    )(x, mod, g, w_qkv, b_qkv)


def _bias_kernel(rpb_ref, o_ref):
    shape = (GRID_W, V7X_LANES)
    qc = lax.broadcasted_iota(jnp.int32, shape, 0)
    lane = lax.broadcasted_iota(jnp.int32, shape, 1)
    head = lane // KEY_COL_BLOCK
    start = jnp.clip(qc - WIN_W // 2, 0, GRID_W - WIN_W)
    for d in range(RPB_ROWS):
        for kb in range(N_KEY_COL_BLOCKS):
            kc = kb * KEY_COL_BLOCK + (lane & (KEY_COL_BLOCK - 1))
            valid = (kc >= start) & (kc < start + WIN_W)
            b = None
            for h in range(HEADS_PER_GROUP):
                r = jnp.broadcast_to(rpb_ref[0, h, d:d + 1, :], shape)
                shift = (KEY_COL_BLOCK * (h - kb) - (WIN_W - 1)) % V7X_LANES
                t = pltpu.roll(r, shift, 1, stride=1, stride_axis=0)
                b = t if b is None else jnp.where(head == h, t, b)
            b = jnp.where(valid, b * LOG2E, NEG_INF)
            lo = SLAB_START[kb]
            o_ref[0, d, kb] = b[lo:lo + SLAB_ROWS]


def _attn_bias_table(rpb):
    r = jnp.pad(rpb.astype(F32), ((0, 0), (0, 0), (0, V7X_LANES - RPB_COLS)))
    r = r.reshape(N_HEAD_GROUPS, HEADS_PER_GROUP, RPB_ROWS, V7X_LANES)
    blk = (1, RPB_ROWS, N_KEY_COL_BLOCKS, SLAB_ROWS, V7X_LANES)
    return pl.pallas_call(
        _bias_kernel,
        out_shape=jax.ShapeDtypeStruct((N_HEAD_GROUPS,) + blk[1:], F32),
        grid=(N_HEAD_GROUPS,),
        in_specs=[pl.BlockSpec((1, HEADS_PER_GROUP, RPB_ROWS, V7X_LANES),
                               lambda g: (g, 0, 0, 0))],
        out_specs=pl.BlockSpec(blk, lambda g: (g, 0, 0, 0, 0)),
        compiler_params=_params(1),
        name="attn_bias_table",
    )(r)


def _key_slot_blocks(u, n_steps):
    first = u == 0
    last = u == n_steps - 1
    return (jnp.where(first, 1, 2 * u - 1),
            2 * u,
            2 * u + 1,
            jnp.where(first, 2, jnp.where(last, 2 * u, 2 * u + 2)))


def _attn_kernel(q_ref, k0_ref, k1_ref, k2_ref, k3_ref, v0_ref, v1_ref, v2_ref, v3_ref,
                 bias_ref, o_ref, s_scr, p_scr, psum_scr, o_scr):
    u = pl.program_id(1)
    k_refs = (k0_ref, k1_ref, k2_ref, k3_ref)
    v_refs = (v0_ref, v1_ref, v2_ref, v3_ref)
    w = V7X_MXU_DIM
    half = V7X_LANES
    kcb = KEY_COL_BLOCK
    slot_off = [KEY_BLOCK_ROWS * blk - Q_ROWS * u + WIN_H - 1
                for blk in _key_slot_blocks(u, pl.num_programs(1))]
    feat_lane = lax.broadcasted_iota(jnp.int32, (kcb, half), 1)
    keep_low = jnp.where(feat_lane < HEAD_DIM, 1.0, 0.0).astype(BF16)
    keep_high = jnp.where(feat_lane < HEAD_DIM, 0.0, 1.0).astype(BF16)
    zero_half = jnp.zeros((kcb, half), BF16)
    col_head = lax.broadcasted_iota(jnp.int32, (GRID_W, half), 1) // kcb
    r_head = (lax.broadcasted_iota(jnp.int32, (w, w), 0) % half) // kcb
    c_head = lax.broadcasted_iota(jnp.int32, (w, w), 1) // HEAD_DIM
    ones_bd = jnp.where(r_head == c_head, 1.0, 0.0).astype(BF16)
    users = [(j, max(0, j - WIN_H + 1), min(Q_ROWS, j + 1)) for j in range(WIN_TILES)]

    @pl.when(u == 0)
    def _():
        p_scr[...] = jnp.zeros(p_scr.shape, BF16)

    def tile(refs, j, lanes):
        slot, row = divmod(j, KEY_BLOCK_ROWS)
        t = refs[slot][0, row * GRID_W:(row + 1) * GRID_W, lanes]
        pieces = []
        for kb in range(N_KEY_COL_BLOCKS):
            tk = t[kb * kcb:(kb + 1) * kcb]
            t01, t23 = tk[:, :half], tk[:, half:]
            pieces += [jnp.concatenate([t01 * keep_low, zero_half], axis=1),
                       jnp.concatenate([t01 * keep_high, zero_half], axis=1),
                       jnp.concatenate([zero_half, t23 * keep_low], axis=1),
                       jnp.concatenate([zero_half, t23 * keep_high], axis=1)]
        return jnp.concatenate(pieces, axis=0)

    def head_max(m1):
        out = None
        for h in range(HEADS_PER_GROUP):
            mh = jnp.max(jnp.where(col_head == h, m1, NEG_INF), axis=-1, keepdims=True)
            out = mh if out is None else jnp.where(col_head == h, mh, out)
        return out

    def scores(g, tiles):
        sl = g % N_ATTN_SLOTS
        lanes = slice(g * w, (g + 1) * w)
        for j, ilo, ihi in tiles:
            kbd = tile(k_refs, j, lanes)
            s = lax.dot_general(q_ref[0, ilo * GRID_W:ihi * GRID_W, lanes], kbd,
                                (((1,), (1,)), ((), ())), preferred_element_type=F32)
            for i in range(ilo, ihi):
                d = slot_off[j // KEY_BLOCK_ROWS] + (j % KEY_BLOCK_ROWS - i)
                r0 = (i - ilo) * GRID_W
                for kb in range(N_KEY_COL_BLOCKS):
                    lo = r0 + SLAB_START[kb]
                    s_scr[sl, i, j - i, kb] = (s[lo:lo + SLAB_ROWS, kb * half:(kb + 1) * half]
                                               + bias_ref[g, d, kb])

    def softmax_numerators(g, i):
        sl = g % N_ATTN_SLOTS
        blocks = range(N_KEY_COL_BLOCKS)
        m1 = [s_scr[sl, i, 0, kb] for kb in blocks]
        for jj in range(1, WIN_H):
            m1 = [jnp.maximum(m1[kb], s_scr[sl, i, jj, kb]) for kb in blocks]
        ov = SLAB_ROWS - SLAB_START[1]
        m_rows = jnp.concatenate([m1[0][:SLAB_START[1]],
                                  jnp.maximum(m1[0][SLAB_START[1]:], m1[1][:ov]),
                                  m1[1][ov:]], axis=0)
        m = head_max(m_rows)
        ms = [m[SLAB_START[kb]:SLAB_START[kb] + SLAB_ROWS] for kb in blocks]
        psum = [jnp.zeros((SLAB_ROWS, half), F32) for _ in blocks]
        pad = jnp.zeros((P_SLAB_ROWS - SLAB_ROWS, half), F32)
        r0 = i * GRID_W
        for jj in range(WIN_H):
            for kb in blocks:
                p = jnp.exp2(s_scr[sl, i, jj, kb] - ms[kb])
                psum[kb] = psum[kb] + p
                wide = jnp.concatenate([p, pad] if kb == 0 else [pad, p], axis=0)
                lo = r0 + P_SLAB_START[kb]
                p_scr[sl, i + jj, lo:lo + P_SLAB_ROWS, kb * half:(kb + 1) * half] = (
                    wide.astype(BF16))
        rest = jnp.zeros((GRID_W - SLAB_ROWS, half), F32)
        psum_scr[sl, r0:r0 + GRID_W, 0:half] = jnp.concatenate([psum[0], rest], axis=0)
        psum_scr[sl, r0:r0 + GRID_W, half:w] = jnp.concatenate([rest, psum[1]], axis=0)

    def weighted_values(g, tiles):
        sl = g % N_ATTN_SLOTS
        lanes = slice(g * w, (g + 1) * w)
        for j, ilo, ihi in tiles:
            vbd = tile(v_refs, j, lanes)
            pv = _dot(p_scr[sl, j, ilo * GRID_W:ihi * GRID_W, :], vbd)
            for i in range(ilo, ihi):
                rows = slice(i * GRID_W, (i + 1) * GRID_W)
                part = pv[(i - ilo) * GRID_W:(i - ilo + 1) * GRID_W]
                if j == i:
                    o_scr[sl, rows, :] = part
                else:
                    o_scr[sl, rows, :] = o_scr[sl, rows, :] + part

    def normalise(g):
        sl = g % N_ATTN_SLOTS
        lanes = slice(g * w, (g + 1) * w)
        ps = psum_scr[sl]
        ps_hi = ps.astype(BF16)
        ps_lo = (ps - ps_hi.astype(F32)).astype(BF16)
        denom = _dot(ps_hi, ones_bd) + _dot(ps_lo, ones_bd)
        o_ref[0, :, lanes] = (o_scr[sl] / denom).astype(BF16)

    for g in range(min(N_ATTN_SLOTS, N_HEAD_GROUPS)):
        scores(g, users)
    for g in range(N_HEAD_GROUPS):
        ahead = g + N_ATTN_SLOTS if g + N_ATTN_SLOTS < N_HEAD_GROUPS else None
        for i in range(Q_ROWS):
            softmax_numerators(g, i)
            weighted_values(g, users[i:i + 1])
            if ahead is not None:
                scores(ahead, users[i:i + 1])
        weighted_values(g, users[Q_ROWS:])
        if ahead is not None:
            scores(ahead, users[Q_ROWS:])
        normalise(g)


def _attention(q, k, v, bias):
    bsz, t, d = q.shape
    assert t % Q_TOK == 0 and t // Q_TOK >= 2
    n_steps = t // Q_TOK
    w = V7X_MXU_DIM
    tok = pl.BlockSpec((1, Q_TOK, d), lambda b, u: (b, u, 0))

    def key_slot(slot):
        return pl.BlockSpec((1, KEY_BLOCK_TOK, d),
                            lambda b, u: (b, _key_slot_blocks(u, n_steps)[slot], 0))

    key_slots = [key_slot(s) for s in range(N_KEY_SLOTS)]
    return pl.pallas_call(
        _attn_kernel,
        out_shape=jax.ShapeDtypeStruct((bsz, t, d), BF16),
        grid=(bsz, n_steps),
        in_specs=[tok] + key_slots + key_slots
                 + [_resident(bias.shape)],
        out_specs=tok,
        scratch_shapes=[
            pltpu.VMEM((N_ATTN_SLOTS, Q_ROWS, WIN_H, N_KEY_COL_BLOCKS, SLAB_ROWS, V7X_LANES),
                       F32),
            pltpu.VMEM((N_ATTN_SLOTS, WIN_TILES, Q_TOK, w), BF16),
            pltpu.VMEM((N_ATTN_SLOTS, Q_TOK, w), F32),
            pltpu.VMEM((N_ATTN_SLOTS, Q_TOK, w), F32),
        ],
        compiler_params=pltpu.CompilerParams(dimension_semantics=("parallel", "arbitrary"),
                                             vmem_limit_bytes=V7X_VMEM_LIMIT_BYTES),
        name="neighbourhood_attention",
    )(q, *([k] * N_KEY_SLOTS), *([v] * N_KEY_SLOTS), bias)


def _halo_specs(t, d, halo=HALO):
    tm = TOKEN_TILE
    per_tile = tm // halo
    n_halo_blocks = t // halo
    main = pl.BlockSpec((1, tm, d), lambda b, i: (b, i, 0))
    prev = pl.BlockSpec((1, halo, d), lambda b, i: (b, jnp.maximum(i * per_tile - 1, 0), 0))
    nxt = pl.BlockSpec(
        (1, halo, d), lambda b, i: (b, jnp.minimum((i + 1) * per_tile, n_halo_blocks - 1), 0))
    return main, prev, nxt


def _sub_tile_rows(prev_ref, main_ref, next_ref, s, halo=HALO):
    lo = s * SUB_TILE
    hi = lo + SUB_TILE
    before = prev_ref[0] if s == 0 else main_ref[0, lo - halo:lo, :]
    after = next_ref[0] if s == N_SUB - 1 else main_ref[0, hi:hi + halo, :]
    return jnp.concatenate([before, main_ref[0, lo:hi, :], after], axis=0)


def _sub_tile_inside(s):
    lo_inside = True if s > 0 else pl.program_id(1) > 0
    hi_inside = True if s < N_SUB - 1 else pl.program_id(1) < pl.num_programs(1) - 1
    return lo_inside, hi_inside


def _run_interleaved(chains):
    live = list(chains)
    while live:
        for chain in list(live):
            try:
                next(chain)
            except StopIteration:
                live.remove(chain)


def _conv3(a, w, lo, n):
    rows = a.shape[0]
    prev = pltpu.roll(a, 1, 0)
    nxt = pltpu.roll(a, rows - 1, 0)
    return prev[lo:lo + n] * w[0:1] + a[lo:lo + n] * w[1:2] + nxt[lo:lo + n] * w[2:3]


def _conv_ffn_stages(x1, halo, inside, mod, g, wup_ref, cw_ref, cb_ref, wdn_ref, out):
    tm = SUB_TILE
    h2 = _modnorm(x1, g, mod[4:5], mod[3:4])
    h2 = _zero_outside_sequence(h2, halo, *inside)
    h2_ext = h2.astype(BF16)
    h2_main = h2[halo:halo + tm].astype(BF16)
    yield

    def up(c0, c1):
        return (_dot(h2_ext, wup_ref[0, :, c0:c1]),
                _dot(h2_main, wup_ref[0, :, D_FF + c0:D_FF + c1]))

    acc = None
    ahead = up(*FFN_CHUNKS[0])
    for c, (c0, c1) in enumerate(FFN_CHUNKS):
        a, val = ahead
        if c + 1 < len(FFN_CHUNKS):
            ahead = up(*FFN_CHUNKS[c + 1])
        yield
        a = _conv3(a, cw_ref[0, :, c0:c1], halo, tm) + cb_ref[0, :, c0:c1]
        gated = (_silu(a) * val).astype(BF16)
        yield
        part = _dot(gated, wdn_ref[0, c0:c1, :])
        acc = part if acc is None else acc + part
    out.append(acc)


def _attn_out_ffn_kernel(xm_ref, xp_ref, xn_ref, om_ref, op_ref, on_ref, mod_ref, g_ref,
                         wo_ref, wup_ref, cw_ref, cb_ref, wdn_ref, out_ref):
    mod = mod_ref[0, 0]

    def chain(s):
        x = _sub_tile_rows(xp_ref, xm_ref, xn_ref, s)
        o = _sub_tile_rows(op_ref, om_ref, on_ref, s)
        x1 = x + mod[2:3] * _dot(o, wo_ref[0])
        yield
        ffn = []
        yield from _conv_ffn_stages(x1, HALO, _sub_tile_inside(s), mod, g_ref[0],
                                    wup_ref, cw_ref, cb_ref, wdn_ref, ffn)
        out_ref[0, s * SUB_TILE:(s + 1) * SUB_TILE, :] = (
            x1[HALO:HALO + SUB_TILE] + mod[5:6] * ffn[0])

    _run_interleaved([chain(s) for s in range(N_SUB)])


def _attn_out_ffn(x, o, mod, seq0, layer, na_layer, g2, w_o, w_up, conv_w, conv_b, w_down):
    bsz, t, d = x.shape
    main, prev, nxt = _halo_specs(t, d)
    return pl.pallas_call(
        _attn_out_ffn_kernel,
        out_shape=jax.ShapeDtypeStruct((bsz, t, d), F32),
        grid=(bsz, t // TOKEN_TILE),
        in_specs=[
            main, prev, nxt, main, prev, nxt,
            pl.BlockSpec((1, 1, 6, d), lambda b, i: (layer, seq0 + b, 0, 0)),
            _layer_block(g2, layer),
            _layer_block(w_o, na_layer),
            _layer_block(w_up, layer),
            _layer_block(conv_w, layer),
            _layer_block(conv_b, layer),
            _layer_block(w_down, layer),
        ],
        out_specs=main,
        compiler_params=_params(2),
        name="attn_out_conv_ffn",
    )(x, x, x, o, o, o, mod, g2, w_o, w_up, conv_w, conv_b, w_down)


def _shortconv_layer_kernel(xm_ref, xp_ref, xn_ref, mod_ref, g1_ref, g2_ref, gf_ref,
                            win_ref, scw_ref, wout_ref, wup_ref, cw_ref, cb_ref, wdn_ref,
                            out_ref):
    tm = SUB_TILE
    d = D_MODEL
    halo = F32_HALO
    n_rows = tm + 2 * halo
    mod = mod_ref[0, 0]
    scw = scw_ref[0]

    def chain(s):
        inside = _sub_tile_inside(s)
        x = _sub_tile_rows(xp_ref, xm_ref, xn_ref, s, halo)
        h1 = _modnorm(x, g1_ref[0], mod[1:2], mod[0:1]).astype(BF16)
        yield
        cg = _dot(h1, win_ref[0, :, d:2 * d])
        uu = _dot(h1, win_ref[0, :, 2 * d:])
        bg = _dot(h1, win_ref[0, :, 0:d])
        yield
        cu = _zero_outside_sequence(cg * uu, halo, *inside)
        z = (bg * _conv3(cu, scw, 0, n_rows)).astype(BF16)
        yield
        x1 = x + mod[2:3] * _dot(z, wout_ref[0])
        yield
        ffn = []
        yield from _conv_ffn_stages(x1, halo, inside, mod, g2_ref[0],
                                    wup_ref, cw_ref, cb_ref, wdn_ref, ffn)
        x2 = x1[halo:halo + tm] + mod[5:6] * ffn[0]
        out_ref[0, s * tm:(s + 1) * tm, :] = _rmsnorm(x2, gf_ref[...])

    _run_interleaved([chain(s) for s in range(N_SUB)])


def _shortconv_layer(x, mod, seq0, layer, sc_layer, g1, g2, gf, w_in, sc_conv_w, w_out,
                     w_up, conv_w, conv_b, w_down):
    bsz, t, d = x.shape
    main, prev, nxt = _halo_specs(t, d, F32_HALO)
    return pl.pallas_call(
        _shortconv_layer_kernel,
        out_shape=jax.ShapeDtypeStruct((bsz, t, d), F32),
        grid=(bsz, t // TOKEN_TILE),
        in_specs=[
            main, prev, nxt,
            pl.BlockSpec((1, 1, 6, d), lambda b, i: (layer, seq0 + b, 0, 0)),
            _layer_block(g1, layer), _layer_block(g2, layer), _resident((1, d)),
            _layer_block(w_in, sc_layer),
            _layer_block(sc_conv_w, sc_layer),
            _layer_block(w_out, sc_layer),
            _layer_block(w_up, layer),
            _layer_block(conv_w, layer),
            _layer_block(conv_b, layer),
            _layer_block(w_down, layer),
        ],
        out_specs=main,
        compiler_params=_params(2),
        name="shortconv_layer",
    )(x, x, x, mod, g1, g2, gf, w_in, sc_conv_w, w_out, w_up, conv_w, conv_b, w_down)


def _trunk(x, seq0, mod, p):
    assert x.shape[1] % TOKEN_TILE == 0 and x.shape[1] % QKV_TILE == 0
    q, k, v = _qkv_proj(x, mod, seq0, 0, 0, p["norm1_g"], p["w_qkv"], p["b_qkv"])
    o = _attention(q, k, v, p["attn_bias"])
    x = _attn_out_ffn(x, o, mod, seq0, 0, 0, p["norm2_g"], p["w_o"], p["w_up"],
                      p["ffn_conv_w"], p["ffn_conv_b"], p["w_down"])
    return _shortconv_layer(x, mod, seq0, 1, 0, p["norm1_g"], p["norm2_g"], p["final_g"],
                            p["w_in"], p["sc_conv_w"], p["w_out"], p["w_up"],
                            p["ffn_conv_w"], p["ffn_conv_b"], p["w_down"])


def kernel(x_prompt, x_sample, c_prompt, c_sample, ada_w, ada_b, norm1_g, norm2_g,
           na_w_qkv, na_b_qkv, na_rpb, na_w_o, sc_w_in, sc_conv_w, sc_w_out,
           ffn_w_up, ffn_conv_w, ffn_conv_b, ffn_w_down, final_g):
    depth = ada_w.shape[0]
    n_prompt = c_prompt.shape[0]
    n_seq = n_prompt + c_sample.shape[0]
    n_pad = -n_seq % 8
    c_all = jnp.concatenate([c_prompt, c_sample, jnp.zeros((n_pad, D_MODEL), F32)], axis=0)
    mod = _modulation(c_all, ada_w, ada_b).reshape(depth, n_seq + n_pad, 6, D_MODEL)
    p = dict(
        norm1_g=norm1_g.reshape(depth, 1, D_MODEL), norm2_g=norm2_g.reshape(depth, 1, D_MODEL),
        final_g=final_g.reshape(1, D_MODEL),
        w_qkv=na_w_qkv.astype(BF16), b_qkv=na_b_qkv.reshape(-1, 1, 3 * D_MODEL),
        attn_bias=_attn_bias_table(na_rpb[0]),
        w_o=na_w_o.astype(BF16),
        w_in=sc_w_in.astype(BF16), sc_conv_w=sc_conv_w, w_out=sc_w_out.astype(BF16),
        w_up=ffn_w_up.astype(BF16), ffn_conv_w=ffn_conv_w,
        ffn_conv_b=ffn_conv_b.reshape(depth, 1, D_FF),
        w_down=ffn_w_down.astype(BF16),
    )
    y_prompt = _trunk(x_prompt, 0, mod, p)
    y_sample = _trunk(x_sample, n_prompt, mod, p)
    return (y_prompt, y_sample)
```
